```python
import jax, jax.numpy as jnp
from jax import lax
import numpy as np

D_MODEL = 1024
BATCH = 16
SEQ = 2048
DEPTH = 1
DEC_BATCH = 8
DEC_SEQ = 64
PAST_LEN = 2048

CHUNK = 64
N_BAND_CHUNKS = 8
PAST_WINDOW = CHUNK * N_BAND_CHUNKS
BAND = PAST_WINDOW + CHUNK
MLP_CHUNK = 128
D_A = D_MODEL // 2
D_B = D_MODEL - D_A
A_GROUPS = 8
A_DG = D_A // A_GROUPS
B_HEADS = 8
B_HD = D_B // B_HEADS
REL_CLIP = 64
N_REL = 2 * REL_CLIP + 1
D_FF = ((8 * D_MODEL // 3 + 255) // 256) * 256
D_PLE = 256
D_IN = 2 * D_A + 3 * D_B
EPS = 1e-6
NEG_INF = -1e30

kernel_name = 'hymba_gmlp_chunkband_stream'


def rmsnorm(x, g):
    xf = x.astype(jnp.float32)
    y = xf * lax.rsqrt(jnp.mean(xf * xf, axis=-1, keepdims=True) + EPS)
    return (y * g.astype(jnp.float32)).astype(x.dtype)


def split_heads(z):
    b, s = z.shape[:2]
    u, v, q, k, vb = jnp.split(z, [D_A, 2 * D_A, 2 * D_A + D_B, 2 * D_A + 2 * D_B], axis=-1)
    u = jax.nn.gelu(u).reshape(b, s, A_GROUPS, A_DG)
    v = jax.nn.gelu(v).reshape(b, s, A_GROUPS, A_DG)
    q = q.reshape(b, s, B_HEADS, B_HD)
    k = k.reshape(b, s, B_HEADS, B_HD)
    vb = vb.reshape(b, s, B_HEADS, B_HD)
    return u, v, q, k, vb


def spatial_gate(u, v, w_s, b_s):
    b, s = u.shape[:2]
    t = w_s.shape[-1]
    w = jnp.where(jnp.tril(jnp.ones((t, t), dtype=bool)), w_s, 0)
    vc = v.reshape(b, s // t, t, A_GROUPS, A_DG)
    mixed = jnp.einsum('gts,bcsgd->bctgd', w, vc) + jnp.transpose(b_s)[None, None, :, :, None]
    return u * mixed.reshape(b, s, A_GROUPS, A_DG)


def rel_bias_lookup(rel_bias, rel):
    return rel_bias[:, jnp.clip(rel, -REL_CLIP, REL_CLIP) + REL_CLIP].astype(jnp.float32)


def attend(q, k, v, bias, valid):
    sc = jnp.einsum('bqhd,bkhd->bhqk', q, k).astype(jnp.float32) * (B_HD ** -0.5) + bias
    sc = jnp.where(valid, sc, NEG_INF)
    p = jax.nn.softmax(sc, axis=-1).astype(v.dtype)
    return jnp.einsum('bhqk,bkhd->bqhd', p, v)


def band_attention_prompt(q, k, v, rel_bias):
    b, s = q.shape[:2]
    nc = s // CHUNK
    pad = ((0, 0), (PAST_WINDOW, 0), (0, 0), (0, 0))
    kp = jnp.pad(k, pad)
    vp = jnp.pad(v, pad)
    qc = jnp.moveaxis(q.reshape(b, nc, CHUNK, B_HEADS, B_HD), 1, 0)
    rel = PAST_WINDOW + jnp.arange(CHUNK)[:, None] - jnp.arange(BAND)[None, :]
    bias = rel_bias_lookup(rel_bias, rel)[None]

    def one_chunk(args):
        c, q_blk = args
        start = c * CHUNK
        k_blk = lax.dynamic_slice_in_dim(kp, start, BAND, axis=1)
        v_blk = lax.dynamic_slice_in_dim(vp, start, BAND, axis=1)
        valid = (start - PAST_WINDOW + jnp.arange(BAND)) >= 0
        return attend(q_blk, k_blk, v_blk, bias, valid[None, None, None, :])

    out = lax.map(one_chunk, (jnp.arange(nc), qc))
    return jnp.moveaxis(out, 0, 1).reshape(b, s, D_B)


def band_attention_sample(q, k, v, k_cache, v_cache, rel_bias):
    b, n = q.shape[:2]
    r = k_cache.shape[1]
    kk = jnp.concatenate([k_cache.astype(k.dtype), k], axis=1)
    vv = jnp.concatenate([v_cache.astype(v.dtype), v], axis=1)
    k_off = jnp.concatenate([jnp.arange(r) - r, jnp.arange(n)])
    rel = jnp.arange(n)[:, None] - k_off[None, :]
    bias = rel_bias_lookup(rel_bias, rel)[None]
    valid = jnp.ones((1, 1, 1, r + n), dtype=bool)
    return attend(q, kk, vv, bias, valid).reshape(b, n, D_B)


def mixer_out(ya, yb, g_out_a, g_out_b, w_out):
    b, s = ya.shape[:2]
    ya = rmsnorm(ya.reshape(b, s, D_A), g_out_a)
    yb = rmsnorm(yb, g_out_b)
    return jnp.concatenate([ya, yb], axis=-1) @ w_out


def channel_and_ple(h, p_l, g_ffn, w_gate, w_up, w_down, g_ple, w_ple_gate, w_ple_proj):
    f = rmsnorm(h, g_ffn)
    h = h + (jax.nn.silu(f @ w_gate) * (f @ w_up)) @ w_down
    gate = jax.nn.sigmoid(rmsnorm(h, g_ple) @ w_ple_gate)
    return h + gate * (p_l.astype(h.dtype) @ w_ple_proj)


def setup_inputs(seed: int = 0) -> dict:
    key = jax.random.key(seed)
    ks = jax.random.split(key, 24)
    f32 = jnp.float32
    r = min(PAST_WINDOW, PAST_LEN)

    def nrm(k, shape, scale):
        return jax.random.normal(k, shape, f32) * scale

    def gain(k, shape):
        return 1.0 + 0.01 * jax.random.normal(k, shape, f32)

    return {
        'x_prompt': nrm(ks[0], (BATCH, SEQ, D_MODEL), 1.0),
        'x_sample': nrm(ks[1], (DEC_BATCH, DEC_SEQ, D_MODEL), 1.0),
        'p_prompt': nrm(ks[2], (DEPTH, BATCH, SEQ, D_PLE), 1.0),
        'p_sample': nrm(ks[3], (DEPTH, DEC_BATCH, DEC_SEQ, D_PLE), 1.0),
        'cache_band_k': nrm(ks[4], (DEPTH, DEC_BATCH, r, B_HEADS, B_HD), 1.0),
        'cache_band_v': nrm(ks[5], (DEPTH, DEC_BATCH, r, B_HEADS, B_HD), 1.0),
        'g_attn': gain(ks[6], (DEPTH, D_MODEL)),
        'w_in': nrm(ks[7], (DEPTH, D_MODEL, D_IN), D_MODEL ** -0.5),
        'g_v': gain(ks[8], (DEPTH, A_GROUPS, A_DG)),
        'w_spatial': nrm(ks[9], (DEPTH, A_GROUPS, MLP_CHUNK, MLP_CHUNK), 0.5 * MLP_CHUNK ** -0.5),
        'b_spatial': gain(ks[10], (DEPTH, A_GROUPS, MLP_CHUNK)),
        'rel_bias': nrm(ks[11], (DEPTH, B_HEADS, N_REL), 0.1),
        'g_out_a': gain(ks[12], (DEPTH, D_A)),
        'g_out_b': gain(ks[13], (DEPTH, D_B)),
        'w_out': nrm(ks[14], (DEPTH, D_MODEL, D_MODEL), D_MODEL ** -0.5),
        'g_ffn': gain(ks[15], (DEPTH, D_MODEL)),
        'w_gate': nrm(ks[16], (DEPTH, D_MODEL, D_FF), D_MODEL ** -0.5),
        'w_up': nrm(ks[17], (DEPTH, D_MODEL, D_FF), D_MODEL ** -0.5),
        'w_down': nrm(ks[18], (DEPTH, D_FF, D_MODEL), D_FF ** -0.5),
        'g_ple': gain(ks[19], (DEPTH, D_MODEL)),
        'w_ple_gate': nrm(ks[20], (DEPTH, D_MODEL, D_MODEL), D_MODEL ** -0.5),
        'w_ple_proj': nrm(ks[21], (DEPTH, D_PLE, D_MODEL), D_PLE ** -0.5),
        'g_final': gain(ks[22], (D_MODEL,)),
    }


def reference(x_prompt, x_sample, p_prompt, p_sample, cache_band_k, cache_band_v,
              g_attn, w_in, g_v, w_spatial, b_spatial, rel_bias, g_out_a, g_out_b, w_out,
              g_ffn, w_gate, w_up, w_down, g_ple, w_ple_gate, w_ple_proj, g_final):
    hp, hs = x_prompt, x_sample
    n = x_sample.shape[1]
    r_p = min(PAST_WINDOW, x_prompt.shape[1])
    kp_l, vp_l, ks_l, vs_l, va_l = [], [], [], [], []
    for l in range(DEPTH):
        up, vp, qp, kp, vbp = split_heads(rmsnorm(hp, g_attn[l]) @ w_in[l])
        vp = rmsnorm(vp, g_v[l])
        ya = spatial_gate(up, vp, w_spatial[l], b_spatial[l])
        yb = band_attention_prompt(qp, kp, vbp, rel_bias[l])
        hp = hp + mixer_out(ya, yb, g_out_a[l], g_out_b[l], w_out[l])
        hp = channel_and_ple(hp, p_prompt[l], g_ffn[l], w_gate[l], w_up[l], w_down[l],
                             g_ple[l], w_ple_gate[l], w_ple_proj[l])
        kp_l.append(kp[:, -r_p:])
        vp_l.append(vbp[:, -r_p:])
        us, vs, qs, ks, vbs = split_heads(rmsnorm(hs, g_attn[l]) @ w_in[l])
        vs = rmsnorm(vs, g_v[l])
        ya = spatial_gate(us, vs, w_spatial[l][:, :n, :n], b_spatial[l][:, :n])
        yb = band_attention_sample(qs, ks, vbs, cache_band_k[l], cache_band_v[l], rel_bias[l])
        hs = hs + mixer_out(ya, yb, g_out_a[l], g_out_b[l], w_out[l])
        hs = channel_and_ple(hs, p_sample[l], g_ffn[l], w_gate[l], w_up[l], w_down[l],
                             g_ple[l], w_ple_gate[l], w_ple_proj[l])
        ks_l.append(ks)
        vs_l.append(vbs)
        va_l.append(vs)
    y_prompt = rmsnorm(hp, g_final)
    y_sample = rmsnorm(hs, g_final)
    return (y_prompt, y_sample, jnp.stack(kp_l), jnp.stack(vp_l), jnp.stack(ks_l), jnp.stack(vs_l), jnp.stack(va_l))
```

```python
import functools

import numpy as np
import jax
import jax.numpy as jnp
from jax import lax
from jax.experimental import pallas as pl
from jax.experimental.pallas import tpu as pltpu

CHUNK = 64
N_BAND_CHUNKS = 8
PAST = CHUNK * N_BAND_CHUNKS
MLP_CHUNK = 128
GROUP = 64
REL_CLIP = 64
EPS = 1e-6
NEG_INF = -1e30
LANES = 128
VMEM_LIMIT = 56 * 1024 * 1024

BF16 = jnp.bfloat16
F32 = jnp.float32
_SQRT_2_OVER_PI = float(np.sqrt(2.0 / np.pi).astype(np.float32))
_WHOLE_VMEM = pl.BlockSpec(memory_space=pltpu.VMEM)


def _rms(x, g):
    ms = jnp.mean(x * x, axis=-1, keepdims=True)
    return x * lax.rsqrt(ms + EPS) * g


def _gelu(x):
    return x * (0.5 * (1.0 + jnp.tanh(_SQRT_2_OVER_PI * (x + 0.044715 * (x * x * x)))))


def _dot(a, b):
    return jnp.dot(a, b, preferred_element_type=F32)


def _dot_nt(a, b):
    return lax.dot_general(a, b, (((1,), (1,)), ((), ())), preferred_element_type=F32)


def _split_heads(x):
    lane = lax.broadcasted_iota(jnp.int32, x.shape, 1)
    even = (lane & GROUP) == 0
    zero = jnp.zeros_like(x)
    return jnp.where(even, x, zero).astype(BF16), jnp.where(even, zero, x).astype(BF16)


def _mixer_kernel(*refs, tile, t_mlp, has_cache, first_tail, emit_vn, d_a, d_b):
    it = iter(refs)
    x_ref = next(it)
    ck_ref = next(it) if has_cache else None
    cv_ref = next(it) if has_cache else None
    g_attn_ref, w_in_ref, g_v_ref, pblk_ref, wsp_ref, bsp_ref, bias_ref = (next(it) for _ in range(7))
    g_oa_ref, g_ob_ref, w_out_ref = (next(it) for _ in range(3))
    h_ref, kt_ref, vt_ref = (next(it) for _ in range(3))
    vn_ref = next(it) if emit_vn else None
    k_scr, v_scr = next(it), next(it)

    t = pl.program_id(1)
    window = PAST + tile

    @pl.when(t == 0)
    def _():
        if has_cache:
            k_scr[0:PAST, :] = ck_ref[0].astype(BF16)
            c_even, c_odd = _split_heads(cv_ref[0])
            v_scr[0, 0:PAST, :] = c_even
            v_scr[1, 0:PAST, :] = c_odd
        else:
            k_scr[0:PAST, :] = jnp.zeros((PAST, d_b), BF16)
            v_scr[0, 0:PAST, :] = jnp.zeros((PAST, d_b), BF16)
            v_scr[1, 0:PAST, :] = jnp.zeros((PAST, d_b), BF16)

    x = x_ref[0]
    a = _rms(x, g_attn_ref[...]).astype(BF16)
    z = _dot(a, w_in_ref[...])
    u = _gelu(z[:, 0:d_a])
    v = _gelu(z[:, d_a:2 * d_a])
    q = z[:, 2 * d_a:2 * d_a + d_b] * (GROUP ** -0.5)
    k = z[:, 2 * d_a + d_b:2 * d_a + 2 * d_b]
    vb = z[:, 2 * d_a + 2 * d_b:2 * d_a + 3 * d_b]

    v2 = (v * v).astype(BF16)
    half = pblk_ref.shape[0]
    ms = jnp.concatenate([_dot(v2[:, c:c + half], pblk_ref[...]) for c in range(0, d_a, half)],
                         axis=1) * (1.0 / GROUP)
    vn = v * lax.rsqrt(ms + EPS) * g_v_ref[...]
    if emit_vn:
        vn_ref[0] = vn

    row0 = pl.multiple_of(t * tile, tile)
    vb_even, vb_odd = _split_heads(vb)
    k_scr[pl.ds(PAST + row0, tile), :] = k.astype(BF16)
    v_scr[0, pl.ds(PAST + row0, tile), :] = vb_even
    v_scr[1, pl.ds(PAST + row0, tile), :] = vb_odd

    @pl.when(t >= first_tail)
    def _():
        kt_ref[0] = k
        vt_ref[0] = vb

    vn_halves = _split_heads(vn)
    nblk = tile // t_mlp
    wrow = lax.broadcasted_iota(jnp.int32, (t_mlp, 2 * t_mlp), 0)
    wcol = lax.broadcasted_iota(jnp.int32, (t_mlp, 2 * t_mlp), 1)
    tril = jnp.where(wcol >= t_mlp, wcol - t_mlp, wcol) <= wrow
    ya_slabs = []
    for j in range(d_a // LANES):
        w = wsp_ref[j]
        w = jnp.where(tril, w, jnp.zeros_like(w)).astype(BF16)
        cols = [jnp.concatenate([vh[blk * t_mlp:(blk + 1) * t_mlp, j * LANES:(j + 1) * LANES]
                                 for vh in vn_halves], axis=0) for blk in range(nblk)]
        rhs = cols[0] if nblk == 1 else jnp.concatenate(cols, axis=1)
        mixed = _dot(w, rhs)
        parts = []
        for blk in range(nblk):
            ub = u[blk * t_mlp:(blk + 1) * t_mlp, j * LANES:(j + 1) * LANES]
            parts.append(ub * (mixed[:, blk * LANES:(blk + 1) * LANES] + bsp_ref[j]))
        ya_slabs.append(parts[0] if nblk == 1 else jnp.concatenate(parts, axis=0))
    ya = jnp.concatenate(ya_slabs, axis=1)

    q_halves = _split_heads(q)
    if not has_cache:
        colw = lax.broadcasted_iota(jnp.int32, (1, window), 1)
        pen = jnp.where(colw >= PAST - row0, 0.0, NEG_INF).astype(F32)
    yb_slabs = []
    for j in range(d_b // LANES):
        kw = k_scr[pl.ds(row0, window), j * LANES:(j + 1) * LANES]
        acc = None
        for hh in range(2):
            qh = q_halves[hh][:, j * LANES:(j + 1) * LANES]
            vh = v_scr[hh, pl.ds(row0, window), j * LANES:(j + 1) * LANES]
            s = _dot_nt(qh, kw) + bias_ref[2 * j + hh]
            if not has_cache:
                s = s + pen
            m = jnp.max(s, axis=-1, keepdims=True)
            p = jnp.exp(s - m)
            l = jnp.sum(p, axis=-1, keepdims=True)
            o = _dot(p.astype(BF16), vh) * (1.0 / l)
            acc = o if acc is None else acc + o
        yb_slabs.append(acc)
    yb = jnp.concatenate(yb_slabs, axis=1)

    ya_n = _rms(ya, g_oa_ref[...]).astype(BF16)
    yb_n = _rms(yb, g_ob_ref[...]).astype(BF16)
    h_ref[0] = x + _dot(ya_n, w_out_ref[0:d_a, :]) + _dot(yb_n, w_out_ref[d_a:d_a + d_b, :])


def _band_bias(rel_bias, tile):
    window = PAST + tile
    r = np.arange(tile)[:, None]
    c = np.arange(window)[None, :]
    idx = np.clip(r - c + PAST, -REL_CLIP, REL_CLIP) + REL_CLIP
    band = (c // CHUNK >= r // CHUNK) & (c // CHUNK <= r // CHUNK + N_BAND_CHUNKS)
    table = rel_bias[:, idx].astype(F32)
    return jnp.where(band[None], table, NEG_INF)


def _mixer(x, cache_k, cache_v, g_attn, w_in_b, g_v, pblk, wsp, bsp, rel_bias, g_oa, g_ob, w_out_b,
           *, tile, t_mlp, tail_rows, emit_vn):
    b, s, d = x.shape
    d_a = g_oa.shape[-1]
    d_b = g_ob.shape[-1]
    has_cache = cache_k is not None
    n_t = s // tile
    first_tail = n_t - tail_rows // tile
    bias = _band_bias(rel_bias, tile)

    def tail_map(bi, ti):
        return (bi, jnp.maximum(ti - first_tail, 0), 0)

    in_specs = [pl.BlockSpec((1, tile, d), lambda bi, ti: (bi, ti, 0))]
    args = [x]
    if has_cache:
        in_specs += [pl.BlockSpec((1, PAST, d_b), lambda bi, ti: (bi, 0, 0))] * 2
        args += [cache_k, cache_v]
    weights = [g_attn, w_in_b, g_v, pblk, wsp, bsp, bias, g_oa, g_ob, w_out_b]
    in_specs += [_WHOLE_VMEM] * len(weights)
    args += weights

    out_shape = [jax.ShapeDtypeStruct((b, s, d), F32),
                 jax.ShapeDtypeStruct((b, tail_rows, d_b), F32),
                 jax.ShapeDtypeStruct((b, tail_rows, d_b), F32)]
    out_specs = [pl.BlockSpec((1, tile, d), lambda bi, ti: (bi, ti, 0)),
                 pl.BlockSpec((1, tile, d_b), tail_map),
                 pl.BlockSpec((1, tile, d_b), tail_map)]
    if emit_vn:
        out_shape.append(jax.ShapeDtypeStruct((b, s, d_a), F32))
        out_specs.append(pl.BlockSpec((1, tile, d_a), lambda bi, ti: (bi, ti, 0)))

    body = functools.partial(_mixer_kernel, tile=tile, t_mlp=t_mlp, has_cache=has_cache,
                             first_tail=first_tail, emit_vn=emit_vn, d_a=d_a, d_b=d_b)
    return pl.pallas_call(
        body,
        grid=(b, n_t),
        in_specs=in_specs,
        out_specs=out_specs,
        out_shape=out_shape,
        scratch_shapes=[pltpu.VMEM((PAST + s, d_b), BF16), pltpu.VMEM((2, PAST + s, d_b), BF16)],
        compiler_params=pltpu.CompilerParams(dimension_semantics=("arbitrary", "arbitrary"),
                                             vmem_limit_bytes=VMEM_LIMIT),
        name="mixer_cache" if has_cache else "mixer",
    )(*args)


def _ffn_kernel(h_ref, p_ref, g_ffn_ref, wg_ref, wu_ref, wd_ref, g_ple_ref, wpg_ref, wpp_ref,
                g_fin_ref, y_ref, *, ff_chunk):
    h = h_ref[...]
    f = _rms(h, g_ffn_ref[...]).astype(BF16)
    d_ff = wg_ref.shape[1]
    acc = h
    for c0 in range(0, d_ff, ff_chunk):
        gate = _dot(f, wg_ref[:, c0:c0 + ff_chunk])
        up = _dot(f, wu_ref[:, c0:c0 + ff_chunk])
        act = (gate * jax.nn.sigmoid(gate) * up).astype(BF16)
        acc = acc + _dot(act, wd_ref[c0:c0 + ff_chunk, :])
    r = _rms(acc, g_ple_ref[...]).astype(BF16)
    pgate = jax.nn.sigmoid(_dot(r, wpg_ref[...]))
    proj = _dot(p_ref[...].astype(BF16), wpp_ref[...])
    y_ref[...] = _rms(acc + pgate * proj, g_fin_ref[...])


def _ffn(h, p, g_ffn, wg_b, wu_b, wd_b, g_ple, wpg_b, wpp_b, g_fin, *, tile_m, ff_chunk):
    m, d = h.shape
    d_p = p.shape[-1]
    weights = [g_ffn, wg_b, wu_b, wd_b, g_ple, wpg_b, wpp_b, g_fin]
    return pl.pallas_call(
        functools.partial(_ffn_kernel, ff_chunk=ff_chunk),
        grid=(m // tile_m,),
        in_specs=[pl.BlockSpec((tile_m, d), lambda i: (i, 0)),
                  pl.BlockSpec((tile_m, d_p), lambda i: (i, 0))] + [_WHOLE_VMEM] * len(weights),
        out_specs=pl.BlockSpec((tile_m, d), lambda i: (i, 0)),
        out_shape=jax.ShapeDtypeStruct((m, d), F32),
        compiler_params=pltpu.CompilerParams(dimension_semantics=("arbitrary",),
                                             vmem_limit_bytes=VMEM_LIMIT),
        name="ffn",
    )(h, p, *weights)


def _pair_spatial(w_s, b_s, t):
    g = w_s.shape[0]
    wsp = w_s.reshape(g // 2, 2, t, t).transpose(0, 2, 1, 3).reshape(g // 2, t, 2 * t)
    bsp = jnp.repeat(b_s.reshape(g // 2, 2, t).transpose(0, 2, 1), GROUP, axis=-1)
    return wsp.astype(F32), bsp.astype(F32)


def kernel(x_prompt, x_sample, p_prompt, p_sample, cache_band_k, cache_band_v, g_attn, w_in, g_v, w_spatial, b_spatial, rel_bias, g_out_a, g_out_b, w_out, g_ffn, w_gate, w_up, w_down, g_ple, w_ple_gate, w_ple_proj, g_final):
    depth = w_in.shape[0]
    assert depth == 1, "the final rmsnorm is fused into the (single) layer's ffn call"
    bp, sp, d = x_prompt.shape
    bs, ss, _ = x_sample.shape
    d_b = g_out_b.shape[-1]
    n_heads = rel_bias.shape[1]
    n_groups = g_v.shape[1]
    r_p = min(PAST, sp)

    half = 2 * LANES
    pblk = (np.arange(half)[:, None] // GROUP == np.arange(half)[None, :] // GROUP)
    pblk = jnp.asarray(pblk, BF16)

    def row(g):
        return g.reshape(1, -1).astype(F32)

    l = 0
    wsp_p, bsp_p = _pair_spatial(w_spatial[l], b_spatial[l], MLP_CHUNK)
    wsp_s, bsp_s = _pair_spatial(w_spatial[l][:, :ss, :ss], b_spatial[l][:, :ss], ss)
    mix_w = (row(g_attn[l]), w_in[l].astype(BF16), row(g_v[l]), pblk)
    out_w = (rel_bias[l], row(g_out_a[l]), row(g_out_b[l]), w_out[l].astype(BF16))
    ffn_w = (row(g_ffn[l]), w_gate[l].astype(BF16), w_up[l].astype(BF16), w_down[l].astype(BF16),
             row(g_ple[l]), w_ple_gate[l].astype(BF16), w_ple_proj[l].astype(BF16), row(g_final))

    hp1, kp, vp = _mixer(x_prompt, None, None, *mix_w, wsp_p, bsp_p, *out_w,
                         tile=256, t_mlp=MLP_CHUNK, tail_rows=r_p, emit_vn=False)
    ck = cache_band_k[l].reshape(bs, PAST, d_b)
    cv = cache_band_v[l].reshape(bs, PAST, d_b)
    hs1, ks, vs, va = _mixer(x_sample, ck, cv, *mix_w, wsp_s, bsp_s, *out_w,
                             tile=ss, t_mlp=ss, tail_rows=ss, emit_vn=True)
    yp = _ffn(hp1.reshape(bp * sp, d), p_prompt[l].reshape(bp * sp, -1), *ffn_w,
              tile_m=512, ff_chunk=1408).reshape(bp, sp, d)
    ys = _ffn(hs1.reshape(bs * ss, d), p_sample[l].reshape(bs * ss, -1), *ffn_w,
              tile_m=bs * ss, ff_chunk=1408).reshape(bs, ss, d)
    return (yp, ys,
            kp.reshape(1, bp, r_p, n_heads, GROUP), vp.reshape(1, bp, r_p, n_heads, GROUP),
            ks.reshape(1, bs, ss, n_heads, GROUP), vs.reshape(1, bs, ss, n_heads, GROUP),
            va.reshape(1, bs, ss, n_groups, GROUP))
```

```python
import functools

import numpy as np
import jax
import jax.numpy as jnp
from jax import lax
from jax.experimental import pallas as pl
from jax.experimental.pallas import tpu as pltpu

CHUNK = 64
N_BAND_CHUNKS = 8
PAST = CHUNK * N_BAND_CHUNKS
MLP_CHUNK = 128
GROUP = 64
REL_CLIP = 64
EPS = 1e-6
NEG_INF = -1e30
LANES = 128
VMEM_LIMIT = 56 * 1024 * 1024

BF16 = jnp.bfloat16
F32 = jnp.float32
_SQRT_2_OVER_PI = float(np.sqrt(2.0 / np.pi).astype(np.float32))
_WHOLE_VMEM = pl.BlockSpec(memory_space=pltpu.VMEM)


def _rms(x, g):
    ms = jnp.mean(x * x, axis=-1, keepdims=True)
    return x * lax.rsqrt(ms + EPS) * g


def _gelu(x):
    return x * (0.5 * (1.0 + jnp.tanh(_SQRT_2_OVER_PI * (x + 0.044715 * (x * x * x)))))


def _dot(a, b):
    return jnp.dot(a, b, preferred_element_type=F32)


def _dot_nt(a, b):
    return lax.dot_general(a, b, (((1,), (1,)), ((), ())), preferred_element_type=F32)


def _split_heads(x):
    lane = lax.broadcasted_iota(jnp.int32, x.shape, 1)
    even = (lane & GROUP) == 0
    zero = jnp.zeros_like(x)
    return jnp.where(even, x, zero).astype(BF16), jnp.where(even, zero, x).astype(BF16)


def _mixer_kernel(*refs, tile, t_mlp, has_cache, first_tail, emit_vn, d_a, d_b):
    it = iter(refs)
    x_ref = next(it)
    ck_ref = next(it) if has_cache else None
    cv_ref = next(it) if has_cache else None
    g_attn_ref, w_in_ref, g_v_ref, pblk_ref, wsp_ref, bsp_ref, relb_ref = (next(it) for _ in range(7))
    g_oa_ref, g_ob_ref, w_out_ref = (next(it) for _ in range(3))
    h_ref, kt_ref, vt_ref = (next(it) for _ in range(3))
    vn_ref = next(it) if emit_vn else None
    k_scr, v_scr = next(it), next(it)
    bias_scr = next(it)

    t = pl.program_id(1)
    window = PAST + tile

    @pl.when((pl.program_id(0) == 0) & (t == 0))
    def _():
        n = relb_ref.shape[-1]
        rc = lax.broadcasted_iota(jnp.int32, (tile, n), 0) >> 6
        cc = lax.broadcasted_iota(jnp.int32, (tile, n), 1) >> 6
        band = (cc >= rc) & (cc <= rc + N_BAND_CHUNKS)
        for h in range(relb_ref.shape[0]):
            base = jnp.broadcast_to(relb_ref[h:h + 1, :], (tile, n))
            toep = pltpu.roll(base, 0, 1, stride=1, stride_axis=0)
            bias_scr[h] = jnp.where(band, toep, NEG_INF)[:, 0:window]

    @pl.when(t == 0)
    def _():
        if has_cache:
            k_scr[0:PAST, :] = ck_ref[0].astype(BF16)
            c_even, c_odd = _split_heads(cv_ref[0])
            v_scr[0, 0:PAST, :] = c_even
            v_scr[1, 0:PAST, :] = c_odd
        else:
            k_scr[0:PAST, :] = jnp.zeros((PAST, d_b), BF16)
            v_scr[0, 0:PAST, :] = jnp.zeros((PAST, d_b), BF16)
            v_scr[1, 0:PAST, :] = jnp.zeros((PAST, d_b), BF16)

    x = x_ref[0]
    a = _rms(x, g_attn_ref[...]).astype(BF16)
    z = _dot(a, w_in_ref[...])
    u = _gelu(z[:, 0:d_a])
    v = _gelu(z[:, d_a:2 * d_a])
    q = z[:, 2 * d_a:2 * d_a + d_b] * (GROUP ** -0.5)
    k = z[:, 2 * d_a + d_b:2 * d_a + 2 * d_b]
    vb = z[:, 2 * d_a + 2 * d_b:2 * d_a + 3 * d_b]

    v2 = (v * v).astype(BF16)
    half = pblk_ref.shape[0]
    ms = jnp.concatenate([_dot(v2[:, c:c + half], pblk_ref[...]) for c in range(0, d_a, half)],
                         axis=1) * (1.0 / GROUP)
    vn = v * lax.rsqrt(ms + EPS) * g_v_ref[...]
    if emit_vn:
        vn_ref[0] = vn

    row0 = pl.multiple_of(t * tile, tile)
    vb_even, vb_odd = _split_heads(vb)
    k_scr[pl.ds(PAST + row0, tile), :] = k.astype(BF16)
    v_scr[0, pl.ds(PAST + row0, tile), :] = vb_even
    v_scr[1, pl.ds(PAST + row0, tile), :] = vb_odd

    @pl.when(t >= first_tail)
    def _():
        kt_ref[0] = k
        vt_ref[0] = vb

    vn_halves = _split_heads(vn)
    nblk = tile // t_mlp
    wrow = lax.broadcasted_iota(jnp.int32, (t_mlp, 2 * t_mlp), 0)
    wcol = lax.broadcasted_iota(jnp.int32, (t_mlp, 2 * t_mlp), 1)
    tril = jnp.where(wcol >= t_mlp, wcol - t_mlp, wcol) <= wrow
    ya_slabs = []
    for j in range(d_a // LANES):
        w = wsp_ref[j]
        w = jnp.where(tril, w, jnp.zeros_like(w)).astype(BF16)
        cols = [jnp.concatenate([vh[blk * t_mlp:(blk + 1) * t_mlp, j * LANES:(j + 1) * LANES]
                                 for vh in vn_halves], axis=0) for blk in range(nblk)]
        rhs = cols[0] if nblk == 1 else jnp.concatenate(cols, axis=1)
        mixed = _dot(w, rhs)
        parts = []
        for blk in range(nblk):
            ub = u[blk * t_mlp:(blk + 1) * t_mlp, j * LANES:(j + 1) * LANES]
            parts.append(ub * (mixed[:, blk * LANES:(blk + 1) * LANES] + bsp_ref[j]))
        ya_slabs.append(parts[0] if nblk == 1 else jnp.concatenate(parts, axis=0))
    ya = jnp.concatenate(ya_slabs, axis=1)

    q_halves = _split_heads(q)
    if not has_cache:
        colw = lax.broadcasted_iota(jnp.int32, (1, window), 1)
        pen = jnp.where(colw >= PAST - row0, 0.0, NEG_INF).astype(F32)
    yb_slabs = []
    for j in range(d_b // LANES):
        kw = k_scr[pl.ds(row0, window), j * LANES:(j + 1) * LANES]
        acc = None
        for hh in range(2):
            qh = q_halves[hh][:, j * LANES:(j + 1) * LANES]
            vh = v_scr[hh, pl.ds(row0, window), j * LANES:(j + 1) * LANES]
            s = _dot_nt(qh, kw) + bias_scr[2 * j + hh]
            if not has_cache:
                s = s + pen
            m = jnp.max(s, axis=-1, keepdims=True)
            p = jnp.exp(s - m)
            l = jnp.sum(p, axis=-1, keepdims=True)
            o = _dot(p.astype(BF16), vh) * (1.0 / l)
            acc = o if acc is None else acc + o
        yb_slabs.append(acc)
    yb = jnp.concatenate(yb_slabs, axis=1)

    ya_n = _rms(ya, g_oa_ref[...]).astype(BF16)
    yb_n = _rms(yb, g_ob_ref[...]).astype(BF16)
    h_ref[0] = x + _dot(ya_n, w_out_ref[0:d_a, :]) + _dot(yb_n, w_out_ref[d_a:d_a + d_b, :])


def _rel_bias_by_offset(rel_bias, tile):
    n = PAST + 2 * tile
    e = np.arange(n)
    e = np.where(e < PAST + tile, e, e - n)
    idx = np.clip(PAST - e, -REL_CLIP, REL_CLIP) + REL_CLIP
    return rel_bias[:, idx].astype(F32)


def _mixer(x, cache_k, cache_v, g_attn, w_in_b, g_v, pblk, wsp, bsp, rel_bias, g_oa, g_ob, w_out_b,
           *, tile, t_mlp, tail_rows, emit_vn):
    b, s, d = x.shape
    d_a = g_oa.shape[-1]
    d_b = g_ob.shape[-1]
    has_cache = cache_k is not None
    n_t = s // tile
    first_tail = n_t - tail_rows // tile
    relb = _rel_bias_by_offset(rel_bias, tile)
    n_heads = rel_bias.shape[0]

    def tail_map(bi, ti):
        return (bi, jnp.maximum(ti - first_tail, 0), 0)

    in_specs = [pl.BlockSpec((1, tile, d), lambda bi, ti: (bi, ti, 0))]
    args = [x]
    if has_cache:
        in_specs += [pl.BlockSpec((1, PAST, d_b), lambda bi, ti: (bi, 0, 0))] * 2
        args += [cache_k, cache_v]
    weights = [g_attn, w_in_b, g_v, pblk, wsp, bsp, relb, g_oa, g_ob, w_out_b]
    in_specs += [_WHOLE_VMEM] * len(weights)
    args += weights

    out_shape = [jax.ShapeDtypeStruct((b, s, d), F32),
                 jax.ShapeDtypeStruct((b, tail_rows, d_b), F32),
                 jax.ShapeDtypeStruct((b, tail_rows, d_b), F32)]
    out_specs = [pl.BlockSpec((1, tile, d), lambda bi, ti: (bi, ti, 0)),
                 pl.BlockSpec((1, tile, d_b), tail_map),
                 pl.BlockSpec((1, tile, d_b), tail_map)]
    if emit_vn:
        out_shape.append(jax.ShapeDtypeStruct((b, s, d_a), F32))
        out_specs.append(pl.BlockSpec((1, tile, d_a), lambda bi, ti: (bi, ti, 0)))

    body = functools.partial(_mixer_kernel, tile=tile, t_mlp=t_mlp, has_cache=has_cache,
                             first_tail=first_tail, emit_vn=emit_vn, d_a=d_a, d_b=d_b)
    return pl.pallas_call(
        body,
        grid=(b, n_t),
        in_specs=in_specs,
        out_specs=out_specs,
        out_shape=out_shape,
        scratch_shapes=[pltpu.VMEM((PAST + s, d_b), BF16), pltpu.VMEM((2, PAST + s, d_b), BF16),
                        pltpu.VMEM((n_heads, tile, PAST + tile), F32)],
        compiler_params=pltpu.CompilerParams(dimension_semantics=("arbitrary", "arbitrary"),
                                             vmem_limit_bytes=VMEM_LIMIT),
        name="mixer_cache" if has_cache else "mixer",
    )(*args)


def _ffn_kernel(h_ref, p_ref, g_ffn_ref, wg_ref, wu_ref, wd_ref, g_ple_ref, wpg_ref, wpp_ref,
                g_fin_ref, y_ref, *, ff_chunk):
    h = h_ref[...]
    f = _rms(h, g_ffn_ref[...]).astype(BF16)
    d_ff = wg_ref.shape[1]
    acc = h
    for c0 in range(0, d_ff, ff_chunk):
        gate = _dot(f, wg_ref[:, c0:c0 + ff_chunk])
        up = _dot(f, wu_ref[:, c0:c0 + ff_chunk])
        act = (gate * jax.nn.sigmoid(gate) * up).astype(BF16)
        acc = acc + _dot(act, wd_ref[c0:c0 + ff_chunk, :])
    r = _rms(acc, g_ple_ref[...]).astype(BF16)
    pgate = jax.nn.sigmoid(_dot(r, wpg_ref[...]))
    proj = _dot(p_ref[...].astype(BF16), wpp_ref[...])
    y_ref[...] = _rms(acc + pgate * proj, g_fin_ref[...])


def _ffn(h, p, g_ffn, wg_b, wu_b, wd_b, g_ple, wpg_b, wpp_b, g_fin, *, tile_m, ff_chunk):
    m, d = h.shape
    d_p = p.shape[-1]
    weights = [g_ffn, wg_b, wu_b, wd_b, g_ple, wpg_b, wpp_b, g_fin]
    return pl.pallas_call(
        functools.partial(_ffn_kernel, ff_chunk=ff_chunk),
        grid=(m // tile_m,),
        in_specs=[pl.BlockSpec((tile_m, d), lambda i: (i, 0)),
                  pl.BlockSpec((tile_m, d_p), lambda i: (i, 0))] + [_WHOLE_VMEM] * len(weights),
        out_specs=pl.BlockSpec((tile_m, d), lambda i: (i, 0)),
        out_shape=jax.ShapeDtypeStruct((m, d), F32),
        compiler_params=pltpu.CompilerParams(dimension_semantics=("arbitrary",),
                                             vmem_limit_bytes=VMEM_LIMIT),
        name="ffn",
    )(h, p, *weights)


def _pair_spatial(w_s, b_s, t):
    g = w_s.shape[0]
    wsp = w_s.reshape(g // 2, 2, t, t).transpose(0, 2, 1, 3).reshape(g // 2, t, 2 * t)
    bsp = jnp.repeat(b_s.reshape(g // 2, 2, t).transpose(0, 2, 1), GROUP, axis=-1)
    return wsp.astype(F32), bsp.astype(F32)


def kernel(x_prompt, x_sample, p_prompt, p_sample, cache_band_k, cache_band_v, g_attn, w_in, g_v, w_spatial, b_spatial, rel_bias, g_out_a, g_out_b, w_out, g_ffn, w_gate, w_up, w_down, g_ple, w_ple_gate, w_ple_proj, g_final):
    depth = w_in.shape[0]
    assert depth == 1, "the final rmsnorm is fused into the (single) layer's ffn call"
    bp, sp, d = x_prompt.shape
    bs, ss, _ = x_sample.shape
    d_b = g_out_b.shape[-1]
    n_heads = rel_bias.shape[1]
    n_groups = g_v.shape[1]
    r_p = min(PAST, sp)

    half = 2 * LANES
    pblk = (np.arange(half)[:, None] // GROUP == np.arange(half)[None, :] // GROUP)
    pblk = jnp.asarray(pblk, BF16)

    def row(g):
        return g.reshape(1, -1).astype(F32)

    l = 0
    wsp_p, bsp_p = _pair_spatial(w_spatial[l], b_spatial[l], MLP_CHUNK)
    wsp_s, bsp_s = _pair_spatial(w_spatial[l][:, :ss, :ss], b_spatial[l][:, :ss], ss)
    mix_w = (row(g_attn[l]), w_in[l].astype(BF16), row(g_v[l]), pblk)
    out_w = (rel_bias[l], row(g_out_a[l]), row(g_out_b[l]), w_out[l].astype(BF16))
    ffn_w = (row(g_ffn[l]), w_gate[l].astype(BF16), w_up[l].astype(BF16), w_down[l].astype(BF16),
             row(g_ple[l]), w_ple_gate[l].astype(BF16), w_ple_proj[l].astype(BF16), row(g_final))

    hp1, kp, vp = _mixer(x_prompt, None, None, *mix_w, wsp_p, bsp_p, *out_w,
                         tile=256, t_mlp=MLP_CHUNK, tail_rows=r_p, emit_vn=False)
    ck = cache_band_k[l].reshape(bs, PAST, d_b)
    cv = cache_band_v[l].reshape(bs, PAST, d_b)
    hs1, ks, vs, va = _mixer(x_sample, ck, cv, *mix_w, wsp_s, bsp_s, *out_w,
                             tile=ss, t_mlp=ss, tail_rows=ss, emit_vn=True)
    yp = _ffn(hp1.reshape(bp * sp, d), p_prompt[l].reshape(bp * sp, -1), *ffn_w,
              tile_m=512, ff_chunk=1408).reshape(bp, sp, d)
    ys = _ffn(hs1.reshape(bs * ss, d), p_sample[l].reshape(bs * ss, -1), *ffn_w,
              tile_m=bs * ss, ff_chunk=1408).reshape(bs, ss, d)
    return (yp, ys,
            kp.reshape(1, bp, r_p, n_heads, GROUP), vp.reshape(1, bp, r_p, n_heads, GROUP),
            ks.reshape(1, bs, ss, n_heads, GROUP), vs.reshape(1, bs, ss, n_heads, GROUP),
            va.reshape(1, bs, ss, n_groups, GROUP))
```

```python
import functools

import numpy as np
import jax
import jax.numpy as jnp
from jax import lax
from jax.experimental import pallas as pl
from jax.experimental.pallas import tpu as pltpu

CHUNK = 64
N_BAND_CHUNKS = 8
PAST = CHUNK * N_BAND_CHUNKS
MLP_CHUNK = 128
GROUP = 64
REL_CLIP = 64
EPS = 1e-6
NEG_INF = -1e30
LANES = 128
VMEM_LIMIT = 60 * 1024 * 1024

BF16 = jnp.bfloat16
F32 = jnp.float32
_SQRT_2_OVER_PI = float(np.sqrt(2.0 / np.pi).astype(np.float32))
_WHOLE_VMEM = pl.BlockSpec(memory_space=pltpu.VMEM)
N_MIX_W = 10
N_FFN_W = 8


def _rms(x, g):
    ms = jnp.mean(x * x, axis=-1, keepdims=True)
    return x * lax.rsqrt(ms + EPS) * g


def _gelu(x):
    return x * (0.5 * (1.0 + jnp.tanh(_SQRT_2_OVER_PI * (x + 0.044715 * (x * x * x)))))


def _dot(a, b):
    return jnp.dot(a, b, preferred_element_type=F32)


def _dot_nt(a, b):
    return lax.dot_general(a, b, (((1,), (1,)), ((), ())), preferred_element_type=F32)


def _even_head_lanes(shape):
    return (lax.broadcasted_iota(jnp.int32, shape, 1) & GROUP) == 0


def _split_heads(x):
    even = _even_head_lanes(x.shape)
    zero = jnp.zeros_like(x)
    return jnp.where(even, x, zero).astype(BF16), jnp.where(even, zero, x).astype(BF16)


def _init_bias(relb_ref, bias_scr, tile):
    n = relb_ref.shape[-1]
    window = PAST + tile
    rc = lax.broadcasted_iota(jnp.int32, (tile, n), 0) >> 6
    cc = lax.broadcasted_iota(jnp.int32, (tile, n), 1) >> 6
    band = (cc >= rc) & (cc <= rc + N_BAND_CHUNKS)
    for h in range(relb_ref.shape[0]):
        base = jnp.broadcast_to(relb_ref[h:h + 1, :], (tile, n))
        toep = pltpu.roll(base, 0, 1, stride=1, stride_axis=0)
        bias_scr[h // 2, (h % 2) * tile:(h % 2 + 1) * tile, :] = jnp.where(band, toep, NEG_INF)[:, 0:window]


def _mixer_tile(x, t, w_refs, k_scr, v_scr, bias_scr, kt_ref, vt_ref, vn_ref, *, tile, t_mlp, mask_pad):
    g_attn_ref, w_in_ref, g_v_ref, pblk_ref, wsp_ref, bsp_ref, _, g_oa_ref, g_ob_ref, w_out_ref = w_refs
    d_a, d_b = g_oa_ref.shape[-1], g_ob_ref.shape[-1]
    window = PAST + tile
    row0 = pl.multiple_of(t * tile, tile)

    a = _rms(x, g_attn_ref[...]).astype(BF16)
    z = _dot(a, w_in_ref[...])
    u = _gelu(z[:, 0:d_a])
    v = _gelu(z[:, d_a:2 * d_a])
    q = z[:, 2 * d_a:2 * d_a + d_b] * (GROUP ** -0.5)
    k = z[:, 2 * d_a + d_b:2 * d_a + 2 * d_b]
    vb = z[:, 2 * d_a + 2 * d_b:2 * d_a + 3 * d_b]

    k_scr[pl.ds(PAST + row0, tile), :] = k.astype(BF16)
    v_scr[pl.ds(PAST + row0, tile), :] = vb.astype(BF16)
    kt_ref[...] = k
    vt_ref[...] = vb

    v2 = (v * v).astype(BF16)
    half = pblk_ref.shape[0]
    ms = jnp.concatenate([_dot(v2[:, c:c + half], pblk_ref[...]) for c in range(0, d_a, half)],
                         axis=1) * (1.0 / GROUP)
    vn = v * lax.rsqrt(ms + EPS) * g_v_ref[...]
    if vn_ref is not None:
        vn_ref[...] = vn

    vn_halves = _split_heads(vn)
    nblk = tile // t_mlp
    wrow = lax.broadcasted_iota(jnp.int32, (t_mlp, 2 * t_mlp), 0)
    wcol = lax.broadcasted_iota(jnp.int32, (t_mlp, 2 * t_mlp), 1)
    tril = jnp.where(wcol >= t_mlp, wcol - t_mlp, wcol) <= wrow
    ya_slabs = []
    for j in range(d_a // LANES):
        w = wsp_ref[j]
        w = jnp.where(tril, w, jnp.zeros_like(w)).astype(BF16)
        cols = [jnp.concatenate([vh[blk * t_mlp:(blk + 1) * t_mlp, j * LANES:(j + 1) * LANES]
                                 for vh in vn_halves], axis=0) for blk in range(nblk)]
        rhs = cols[0] if nblk == 1 else jnp.concatenate(cols, axis=1)
        mixed = _dot(w, rhs)
        parts = []
        for blk in range(nblk):
            ub = u[blk * t_mlp:(blk + 1) * t_mlp, j * LANES:(j + 1) * LANES]
            parts.append(ub * (mixed[:, blk * LANES:(blk + 1) * LANES] + bsp_ref[j]))
        ya_slabs.append(parts[0] if nblk == 1 else jnp.concatenate(parts, axis=0))
    ya = jnp.concatenate(ya_slabs, axis=1)

    q_halves = _split_heads(q)
    even_lane = _even_head_lanes((tile, LANES))
    if mask_pad:
        colw = lax.broadcasted_iota(jnp.int32, (1, window), 1)
        pen = jnp.where(colw >= PAST - row0, 0.0, NEG_INF).astype(F32)
    yb_slabs = []
    for j in range(d_b // LANES):
        kw = k_scr[pl.ds(row0, window), j * LANES:(j + 1) * LANES]
        vw = v_scr[pl.ds(row0, window), j * LANES:(j + 1) * LANES]
        qp = jnp.concatenate([qh[:, j * LANES:(j + 1) * LANES] for qh in q_halves], axis=0)
        s = _dot_nt(qp, kw) + bias_scr[j]
        if mask_pad:
            s = s + pen
        m = jnp.max(s, axis=-1, keepdims=True)
        p = jnp.exp(s - m)
        l = jnp.sum(p, axis=-1, keepdims=True)
        o = _dot(p.astype(BF16), vw) * (1.0 / l)
        yb_slabs.append(jnp.where(even_lane, o[0:tile], o[tile:2 * tile]))
    yb = jnp.concatenate(yb_slabs, axis=1)

    ya_n = _rms(ya, g_oa_ref[...]).astype(BF16)
    yb_n = _rms(yb, g_ob_ref[...]).astype(BF16)
    return x + _dot(ya_n, w_out_ref[0:d_a, :]) + _dot(yb_n, w_out_ref[d_a:d_a + d_b, :])


def _ffn_tile(h, p, w_refs, *, ff_chunk):
    g_ffn_ref, wg_ref, wu_ref, wd_ref, g_ple_ref, wpg_ref, wpp_ref, g_fin_ref = w_refs
    f = _rms(h, g_ffn_ref[...]).astype(BF16)
    d_ff = wg_ref.shape[1]
    acc = h
    for c0 in range(0, d_ff, ff_chunk):
        gate = _dot(f, wg_ref[:, c0:c0 + ff_chunk])
        up = _dot(f, wu_ref[:, c0:c0 + ff_chunk])
        act = (gate * jax.nn.sigmoid(gate) * up).astype(BF16)
        acc = acc + _dot(act, wd_ref[c0:c0 + ff_chunk, :])
    r = _rms(acc, g_ple_ref[...]).astype(BF16)
    pgate = jax.nn.sigmoid(_dot(r, wpg_ref[...]))
    proj = _dot(p.astype(BF16), wpp_ref[...])
    return _rms(acc + pgate * proj, g_fin_ref[...])


def _fill_past(k_scr, v_scr, ck_ref, cv_ref):
    d_b = k_scr.shape[-1]
    if ck_ref is None:
        k_scr[0:PAST, :] = jnp.zeros((PAST, d_b), BF16)
        v_scr[0:PAST, :] = jnp.zeros((PAST, d_b), BF16)
    else:
        k_scr[0:PAST, :] = ck_ref[0].astype(BF16)
        v_scr[0:PAST, :] = cv_ref[0].astype(BF16)


def _mixer_kernel(*refs, tile, t_mlp, n_t, has_cache, emit_vn):
    it = iter(refs)
    x_ref = next(it)
    ck_ref = next(it) if has_cache else None
    cv_ref = next(it) if has_cache else None
    w_refs = tuple(next(it) for _ in range(N_MIX_W))
    h_ref, kt_ref, vt_ref = (next(it) for _ in range(3))
    vn_ref = next(it) if emit_vn else None
    k_scr, v_scr, bias_scr = (next(it) for _ in range(3))

    i = pl.program_id(0)
    t = lax.rem(i, n_t)

    @pl.when(i == 0)
    def _():
        _init_bias(w_refs[6], bias_scr, tile)

    @pl.when(t == 0)
    def _():
        _fill_past(k_scr, v_scr, ck_ref, cv_ref)

    h_ref[...] = _mixer_tile(x_ref[...], t, w_refs, k_scr, v_scr, bias_scr, kt_ref, vt_ref, vn_ref,
                             tile=tile, t_mlp=t_mlp, mask_pad=not has_cache)


def _ffn_kernel(h_ref, p_ref, *refs, ff_chunk):
    y_ref = refs[N_FFN_W]
    y_ref[...] = _ffn_tile(h_ref[...], p_ref[...], refs[:N_FFN_W], ff_chunk=ff_chunk)


def _layer_kernel(x_ref, p_ref, *refs, tile, t_mlp, n_t, n_tiles, ff_chunk):
    mix_w = refs[:N_MIX_W]
    ffn_w = refs[N_MIX_W:N_MIX_W + N_FFN_W]
    y_ref, kt_ref, vt_ref, k_scr, v_scr, bias_scr, h_scr = refs[N_MIX_W + N_FFN_W:]

    i = pl.program_id(0)
    t = lax.rem(jnp.minimum(i, n_tiles - 1), n_t)

    @pl.when(i == 0)
    def _():
        _init_bias(mix_w[6], bias_scr, tile)
        h_scr[1] = jnp.zeros(h_scr.shape[1:], F32)

    @pl.when(t == 0)
    def _():
        _fill_past(k_scr, v_scr, None, None)

    h_prev = h_scr[lax.rem(i + 1, 2)]
    h_scr[lax.rem(i, 2)] = _mixer_tile(x_ref[...], t, mix_w, k_scr, v_scr, bias_scr, kt_ref, vt_ref, None,
                                       tile=tile, t_mlp=t_mlp, mask_pad=True)
    y_ref[...] = _ffn_tile(h_prev, p_ref[...], ffn_w, ff_chunk=ff_chunk)


def _rel_bias_by_offset(rel_bias, tile):
    n = PAST + 2 * tile
    e = np.arange(n)
    e = np.where(e < PAST + tile, e, e - n)
    idx = np.clip(PAST - e, -REL_CLIP, REL_CLIP) + REL_CLIP
    return rel_bias[:, idx].astype(F32)


def _tail_map(n_t, tail_tiles):
    first_tail = n_t - tail_tiles

    def index(i):
        return (i // n_t) * tail_tiles + jnp.maximum(lax.rem(i, n_t) - first_tail, 0), 0
    return index


def _scratch(s, tile, d_b, n_heads):
    return [pltpu.VMEM((PAST + s, d_b), BF16), pltpu.VMEM((PAST + s, d_b), BF16),
            pltpu.VMEM((n_heads // 2, 2 * tile, PAST + tile), F32)]


def _mixer(x, cache_k, cache_v, mix_w, *, tile, t_mlp, tail_rows, emit_vn):
    b, s, d = x.shape
    d_a, d_b = mix_w[7].shape[-1], mix_w[8].shape[-1]
    n_heads = mix_w[6].shape[0]
    has_cache = cache_k is not None
    n_t = s // tile
    tail = _tail_map(n_t, tail_rows // tile)

    in_specs = [pl.BlockSpec((tile, d), lambda i: (i, 0))]
    args = [x.reshape(b * s, d)]
    if has_cache:
        in_specs += [pl.BlockSpec((1, PAST, d_b), lambda i: (i // n_t, 0, 0))] * 2
        args += [cache_k, cache_v]
    in_specs += [_WHOLE_VMEM] * N_MIX_W
    args += list(mix_w)

    out_shape = [jax.ShapeDtypeStruct((b * s, d), F32),
                 jax.ShapeDtypeStruct((b * tail_rows, d_b), F32),
                 jax.ShapeDtypeStruct((b * tail_rows, d_b), F32)]
    out_specs = [pl.BlockSpec((tile, d), lambda i: (i, 0)),
                 pl.BlockSpec((tile, d_b), tail), pl.BlockSpec((tile, d_b), tail)]
    if emit_vn:
        out_shape.append(jax.ShapeDtypeStruct((b * s, d_a), F32))
        out_specs.append(pl.BlockSpec((tile, d_a), lambda i: (i, 0)))

    return pl.pallas_call(
        functools.partial(_mixer_kernel, tile=tile, t_mlp=t_mlp, n_t=n_t, has_cache=has_cache, emit_vn=emit_vn),
        grid=(b * n_t,),
        in_specs=in_specs,
        out_specs=out_specs,
        out_shape=out_shape,
        scratch_shapes=_scratch(s, tile, d_b, n_heads),
        compiler_params=pltpu.CompilerParams(dimension_semantics=("arbitrary",), vmem_limit_bytes=VMEM_LIMIT),
        name="mixer",
    )(*args)


def _ffn(h, p, ffn_w, *, tile_m, ff_chunk):
    m, d = h.shape
    d_p = p.shape[-1]
    return pl.pallas_call(
        functools.partial(_ffn_kernel, ff_chunk=ff_chunk),
        grid=(m // tile_m,),
        in_specs=[pl.BlockSpec((tile_m, d), lambda i: (i, 0)),
                  pl.BlockSpec((tile_m, d_p), lambda i: (i, 0))] + [_WHOLE_VMEM] * N_FFN_W,
        out_specs=pl.BlockSpec((tile_m, d), lambda i: (i, 0)),
        out_shape=jax.ShapeDtypeStruct((m, d), F32),
        compiler_params=pltpu.CompilerParams(dimension_semantics=("arbitrary",), vmem_limit_bytes=VMEM_LIMIT),
        name="ffn",
    )(h, p, *ffn_w)


def _layer(x, p, mix_w, ffn_w, *, tile, t_mlp, tail_rows, ff_chunk):
    b, s, d = x.shape
    d_p = p.shape[-1]
    d_b = mix_w[8].shape[-1]
    n_heads = mix_w[6].shape[0]
    n_t = s // tile
    n_tiles = b * n_t
    tail = _tail_map(n_t, tail_rows // tile)

    def mix_tile(i):
        return jnp.minimum(i, n_tiles - 1)

    def ffn_block(i):
        return jnp.maximum(i - 1, 0), 0

    return pl.pallas_call(
        functools.partial(_layer_kernel, tile=tile, t_mlp=t_mlp, n_t=n_t, n_tiles=n_tiles, ff_chunk=ff_chunk),
        grid=(n_tiles + 1,),
        in_specs=[pl.BlockSpec((tile, d), lambda i: (mix_tile(i), 0)), pl.BlockSpec((tile, d_p), ffn_block)]
        + [_WHOLE_VMEM] * (N_MIX_W + N_FFN_W),
        out_specs=[pl.BlockSpec((tile, d), ffn_block),
                   pl.BlockSpec((tile, d_b), lambda i: tail(mix_tile(i))),
                   pl.BlockSpec((tile, d_b), lambda i: tail(mix_tile(i)))],
        out_shape=[jax.ShapeDtypeStruct((b * s, d), F32),
                   jax.ShapeDtypeStruct((b * tail_rows, d_b), F32),
                   jax.ShapeDtypeStruct((b * tail_rows, d_b), F32)],
        scratch_shapes=_scratch(s, tile, d_b, n_heads) + [pltpu.VMEM((2, tile, d), F32)],
        compiler_params=pltpu.CompilerParams(dimension_semantics=("arbitrary",), vmem_limit_bytes=VMEM_LIMIT),
        name="layer",
    )(x.reshape(b * s, d), p.reshape(b * s, d_p), *mix_w, *ffn_w)


def _pair_spatial(w_s, b_s, t):
    g = w_s.shape[0]
    wsp = w_s.reshape(g // 2, 2, t, t).transpose(0, 2, 1, 3).reshape(g // 2, t, 2 * t)
    bsp = jnp.repeat(b_s.reshape(g // 2, 2, t).transpose(0, 2, 1), GROUP, axis=-1)
    return wsp.astype(F32), bsp.astype(F32)


def kernel(x_prompt, x_sample, p_prompt, p_sample, cache_band_k, cache_band_v, g_attn, w_in, g_v, w_spatial, b_spatial, rel_bias, g_out_a, g_out_b, w_out, g_ffn, w_gate, w_up, w_down, g_ple, w_ple_gate, w_ple_proj, g_final):
    depth = w_in.shape[0]
    assert depth == 1, "the final rmsnorm is fused into the (single) layer's ffn stage"
    bp, sp, d = x_prompt.shape
    bs, ss, _ = x_sample.shape
    d_b = g_out_b.shape[-1]
    n_heads = rel_bias.shape[1]
    n_groups = g_v.shape[1]
    r_p = min(PAST, sp)
    tile_p = 256

    half = 2 * LANES
    pblk = (np.arange(half)[:, None] // GROUP == np.arange(half)[None, :] // GROUP)
    pblk = jnp.asarray(pblk, BF16)

    def row(g):
        return g.reshape(1, -1).astype(F32)

    def mixer_weights(t_mlp, tile):
        wsp, bsp = _pair_spatial(w_spatial[0][:, :t_mlp, :t_mlp], b_spatial[0][:, :t_mlp], t_mlp)
        return (row(g_attn[0]), w_in[0].astype(BF16), row(g_v[0]), pblk, wsp, bsp,
                _rel_bias_by_offset(rel_bias[0], tile), row(g_out_a[0]), row(g_out_b[0]), w_out[0].astype(BF16))

    ffn_w = (row(g_ffn[0]), w_gate[0].astype(BF16), w_up[0].astype(BF16), w_down[0].astype(BF16),
             row(g_ple[0]), w_ple_gate[0].astype(BF16), w_ple_proj[0].astype(BF16), row(g_final))

    yp, kp, vp = _layer(x_prompt, p_prompt[0], mixer_weights(MLP_CHUNK, tile_p), ffn_w,
                        tile=tile_p, t_mlp=MLP_CHUNK, tail_rows=r_p, ff_chunk=1408)
    ck = cache_band_k[0].reshape(bs, PAST, d_b)
    cv = cache_band_v[0].reshape(bs, PAST, d_b)
    hs1, ks, vs, va = _mixer(x_sample, ck, cv, mixer_weights(ss, ss),
                             tile=ss, t_mlp=ss, tail_rows=ss, emit_vn=True)
    ys = _ffn(hs1, p_sample[0].reshape(bs * ss, -1), ffn_w, tile_m=bs * ss, ff_chunk=1408)
    return (yp.reshape(bp, sp, d), ys.reshape(bs, ss, d),
            kp.reshape(1, bp, r_p, n_heads, GROUP), vp.reshape(1, bp, r_p, n_heads, GROUP),
            ks.reshape(1, bs, ss, n_heads, GROUP), vs.reshape(1, bs, ss, n_heads, GROUP),
            va.reshape(1, bs, ss, n_groups, GROUP))
```

```python
import functools

import numpy as np
import jax
import jax.numpy as jnp
from jax import lax
from jax.experimental import pallas as pl
from jax.experimental.pallas import tpu as pltpu

CHUNK = 64
N_BAND_CHUNKS = 8
PAST = CHUNK * N_BAND_CHUNKS
MLP_CHUNK = 128
GROUP = 64
REL_CLIP = 64
EPS = 1e-6
NEG_INF = -1e30
LANES = 128
VMEM_LIMIT = 60 * 1024 * 1024

BF16 = jnp.bfloat16
F32 = jnp.float32
_SQRT_2_OVER_PI = float(np.sqrt(2.0 / np.pi).astype(np.float32))
_WHOLE_VMEM = pl.BlockSpec(memory_space=pltpu.VMEM)
N_MIX_W = 10
N_FFN_W = 8


def _inv_rms(x):
    return lax.rsqrt(jnp.mean(x * x, axis=-1, keepdims=True) + EPS)


def _rms(x, g):
    return x * _inv_rms(x) * g


def _rms_dot(x, g, w):
    return _dot((x * g).astype(BF16), w) * _inv_rms(x)


def _gelu(x):
    return x * (0.5 * (1.0 + jnp.tanh(_SQRT_2_OVER_PI * (x + 0.044715 * (x * x * x)))))


def _dot(a, b):
    return jnp.dot(a, b, preferred_element_type=F32)


def _dot_nt(a, b):
    return lax.dot_general(a, b, (((1,), (1,)), ((), ())), preferred_element_type=F32)


def _even_head_lanes(shape):
    return (lax.broadcasted_iota(jnp.int32, shape, 1) & GROUP) == 0


def _split_heads(x):
    even = _even_head_lanes(x.shape)
    zero = jnp.zeros_like(x)
    return jnp.where(even, x, zero).astype(BF16), jnp.where(even, zero, x).astype(BF16)


def _init_bias(relb_ref, bias_scr, tile):
    n = relb_ref.shape[-1]
    window = PAST + tile
    rc = lax.broadcasted_iota(jnp.int32, (tile, n), 0) >> 6
    cc = lax.broadcasted_iota(jnp.int32, (tile, n), 1) >> 6
    band = (cc >= rc) & (cc <= rc + N_BAND_CHUNKS)
    for h in range(relb_ref.shape[0]):
        base = jnp.broadcast_to(relb_ref[h:h + 1, :], (tile, n))
        toep = pltpu.roll(base, 0, 1, stride=1, stride_axis=0)
        bias_scr[h // 2, (h % 2) * tile:(h % 2 + 1) * tile, :] = jnp.where(band, toep, NEG_INF)[:, 0:window]


def _mixer_tile(x, t, w_refs, k_scr, v_scr, bias_scr, kt_ref, vt_ref, vn_ref, *, tile, t_mlp, mask_pad,
                after_scores=lambda: None):
    g_attn_ref, w_in_ref, g_v_ref, pblk_ref, wsp_ref, bsp_ref, _, g_oa_ref, g_ob_ref, w_out_ref = w_refs
    d_a, d_b = g_oa_ref.shape[-1], g_ob_ref.shape[-1]
    window = PAST + tile
    row0 = pl.multiple_of(t * tile, tile)

    z = _rms_dot(x, g_attn_ref[...], w_in_ref[...])
    u = _gelu(z[:, 0:d_a])
    v = _gelu(z[:, d_a:2 * d_a])
    q = z[:, 2 * d_a:2 * d_a + d_b] * (GROUP ** -0.5)
    k = z[:, 2 * d_a + d_b:2 * d_a + 2 * d_b]
    vb = z[:, 2 * d_a + 2 * d_b:2 * d_a + 3 * d_b]

    k_scr[pl.ds(PAST + row0, tile), :] = k.astype(BF16)
    v_scr[pl.ds(PAST + row0, tile), :] = vb.astype(BF16)
    kt_ref[...] = k.T
    vt_ref[...] = vb.T

    v2 = (v * v).astype(BF16)
    half = pblk_ref.shape[0]
    ms = jnp.concatenate([_dot(v2[:, c:c + half], pblk_ref[...]) for c in range(0, d_a, half)],
                         axis=1) * (1.0 / GROUP)
    vn = v * lax.rsqrt(ms + EPS) * g_v_ref[...]
    if vn_ref is not None:
        vn_ref[...] = vn

    vn_halves = _split_heads(vn)
    nblk = tile // t_mlp
    wrow = lax.broadcasted_iota(jnp.int32, (t_mlp, 2 * t_mlp), 0)
    wcol = lax.broadcasted_iota(jnp.int32, (t_mlp, 2 * t_mlp), 1)
    tril = jnp.where(wcol >= t_mlp, wcol - t_mlp, wcol) <= wrow
    ya_slabs = []
    for j in range(d_a // LANES):
        w = wsp_ref[j]
        w = jnp.where(tril, w, jnp.zeros_like(w)).astype(BF16)
        cols = [jnp.concatenate([vh[blk * t_mlp:(blk + 1) * t_mlp, j * LANES:(j + 1) * LANES]
                                 for vh in vn_halves], axis=0) for blk in range(nblk)]
        rhs = cols[0] if nblk == 1 else jnp.concatenate(cols, axis=1)
        mixed = _dot(w, rhs)
        parts = []
        for blk in range(nblk):
            ub = u[blk * t_mlp:(blk + 1) * t_mlp, j * LANES:(j + 1) * LANES]
            parts.append(ub * (mixed[:, blk * LANES:(blk + 1) * LANES] + bsp_ref[j]))
        ya_slabs.append(parts[0] if nblk == 1 else jnp.concatenate(parts, axis=0))
    ya = jnp.concatenate(ya_slabs, axis=1)

    q_halves = _split_heads(q)
    even_lane = _even_head_lanes((tile, LANES))
    if mask_pad:
        colw = lax.broadcasted_iota(jnp.int32, (1, window), 1)
        pen = jnp.where(colw >= PAST - row0, 0.0, NEG_INF).astype(F32)
    yb_slabs = []
    for j in range(d_b // LANES):
        kw = k_scr[pl.ds(row0, window), j * LANES:(j + 1) * LANES]
        vw = v_scr[pl.ds(row0, window), j * LANES:(j + 1) * LANES]
        qp = jnp.concatenate([qh[:, j * LANES:(j + 1) * LANES] for qh in q_halves], axis=0)
        s = _dot_nt(qp, kw)
        after_scores()
        s = s + bias_scr[j]
        if mask_pad:
            s = s + pen
        m = jnp.max(s, axis=-1, keepdims=True)
        p = jnp.exp(s - m)
        l = jnp.sum(p, axis=-1, keepdims=True)
        o = _dot(p.astype(BF16), vw) * (1.0 / l)
        yb_slabs.append(jnp.where(even_lane, o[0:tile], o[tile:2 * tile]))
    yb = jnp.concatenate(yb_slabs, axis=1)

    return (x + _rms_dot(ya, g_oa_ref[...], w_out_ref[0:d_a, :])
            + _rms_dot(yb, g_ob_ref[...], w_out_ref[d_a:d_a + d_b, :]))


def _ffn_stages(h, p, w_refs, *, ff_chunk):
    g_ffn_ref, wg_ref, wu_ref, wd_ref, g_ple_ref, wpg_ref, wpp_ref, g_fin_ref = w_refs
    f = (h * g_ffn_ref[...]).astype(BF16)
    r = _inv_rms(h)
    d_ff = wg_ref.shape[1]
    acc = h
    act_prev = None
    for c0 in range(0, d_ff, ff_chunk):
        c1 = min(c0 + ff_chunk, d_ff)
        gate = _dot(f, wg_ref[:, c0:c1]) * r
        up = _dot(f, wu_ref[:, c0:c1]) * r
        if act_prev is not None:
            acc = acc + _dot(act_prev, wd_ref[c0 - ff_chunk:c0, :])
        act_prev = (gate * jax.nn.sigmoid(gate) * up).astype(BF16)
        yield
    acc = acc + _dot(act_prev, wd_ref[c0:d_ff, :])
    pgate = jax.nn.sigmoid(_rms_dot(acc, g_ple_ref[...], wpg_ref[...]))
    proj = _dot(p.astype(BF16), wpp_ref[...])
    return _rms(acc + pgate * proj, g_fin_ref[...])


def _advance(stages):
    try:
        next(stages)
    except StopIteration as done:
        return done.value
    return None


def _ffn_tile(h, p, w_refs, *, ff_chunk):
    stages = _ffn_stages(h, p, w_refs, ff_chunk=ff_chunk)
    while True:
        y = _advance(stages)
        if y is not None:
            return y


def _fill_past(k_scr, v_scr, ck_ref, cv_ref):
    d_b = k_scr.shape[-1]
    if ck_ref is None:
        k_scr[0:PAST, :] = jnp.zeros((PAST, d_b), BF16)
        v_scr[0:PAST, :] = jnp.zeros((PAST, d_b), BF16)
    else:
        k_scr[0:PAST, :] = ck_ref[0].astype(BF16)
        v_scr[0:PAST, :] = cv_ref[0].astype(BF16)


def _mixer_kernel(*refs, tile, t_mlp, n_t, has_cache, emit_vn):
    it = iter(refs)
    x_ref = next(it)
    ck_ref = next(it) if has_cache else None
    cv_ref = next(it) if has_cache else None
    w_refs = tuple(next(it) for _ in range(N_MIX_W))
    h_ref, kt_ref, vt_ref = (next(it) for _ in range(3))
    vn_ref = next(it) if emit_vn else None
    k_scr, v_scr, bias_scr = (next(it) for _ in range(3))

    i = pl.program_id(0)
    t = lax.rem(i, n_t)

    @pl.when(i == 0)
    def _():
        _init_bias(w_refs[6], bias_scr, tile)

    @pl.when(t == 0)
    def _():
        _fill_past(k_scr, v_scr, ck_ref, cv_ref)

    h_ref[...] = _mixer_tile(x_ref[...], t, w_refs, k_scr, v_scr, bias_scr, kt_ref, vt_ref, vn_ref,
                             tile=tile, t_mlp=t_mlp, mask_pad=not has_cache)


def _ffn_kernel(h_ref, p_ref, *refs, ff_chunk):
    y_ref = refs[N_FFN_W]
    y_ref[...] = _ffn_tile(h_ref[...], p_ref[...], refs[:N_FFN_W], ff_chunk=ff_chunk)


def _layer_kernel(x_ref, p_ref, *refs, tile, t_mlp, n_t, n_tiles, ff_chunk):
    mix_w = refs[:N_MIX_W]
    ffn_w = refs[N_MIX_W:N_MIX_W + N_FFN_W]
    y_ref, kt_ref, vt_ref, k_scr, v_scr, bias_scr, h_scr = refs[N_MIX_W + N_FFN_W:]

    i = pl.program_id(0)
    t = lax.rem(jnp.minimum(i, n_tiles - 1), n_t)

    @pl.when(i == 0)
    def _():
        _init_bias(mix_w[6], bias_scr, tile)
        h_scr[1] = jnp.zeros(h_scr.shape[1:], F32)

    @pl.when(t == 0)
    def _():
        _fill_past(k_scr, v_scr, None, None)

    h_prev = h_scr[lax.rem(i + 1, 2)]
    stages = _ffn_stages(h_prev, p_ref[...], ffn_w, ff_chunk=ff_chunk)
    h_scr[lax.rem(i, 2)] = _mixer_tile(x_ref[...], t, mix_w, k_scr, v_scr, bias_scr, kt_ref, vt_ref, None,
                                       tile=tile, t_mlp=t_mlp, mask_pad=True,
                                       after_scores=lambda: _advance(stages))
    y = None
    while y is None:
        y = _advance(stages)
    y_ref[...] = y


def _rel_bias_by_offset(rel_bias, tile):
    n = PAST + 2 * tile
    e = np.arange(n)
    e = np.where(e < PAST + tile, e, e - n)
    idx = np.clip(PAST - e, -REL_CLIP, REL_CLIP) + REL_CLIP
    return rel_bias[:, idx].astype(F32)


def _tail_map(n_t, tail_tiles):
    first_tail = n_t - tail_tiles

    def index(i):
        return i // n_t, jnp.maximum(lax.rem(i, n_t) - first_tail, 0)
    return index


def _scratch(s, tile, d_b, n_heads):
    return [pltpu.VMEM((PAST + s, d_b), BF16), pltpu.VMEM((PAST + s, d_b), BF16),
            pltpu.VMEM((n_heads // 2, 2 * tile, PAST + tile), F32)]


def _mixer(x, cache_k, cache_v, mix_w, *, tile, t_mlp, tail_rows, emit_vn):
    b, s, d = x.shape
    d_a, d_b = mix_w[7].shape[-1], mix_w[8].shape[-1]
    n_heads = mix_w[6].shape[0]
    has_cache = cache_k is not None
    n_t = s // tile
    tail = _tail_map(n_t, tail_rows // tile)

    in_specs = [pl.BlockSpec((tile, d), lambda i: (i, 0))]
    args = [x.reshape(b * s, d)]
    if has_cache:
        in_specs += [pl.BlockSpec((1, PAST, d_b), lambda i: (i // n_t, 0, 0))] * 2
        args += [cache_k, cache_v]
    in_specs += [_WHOLE_VMEM] * N_MIX_W
    args += list(mix_w)

    out_shape = [jax.ShapeDtypeStruct((b * s, d), F32),
                 jax.ShapeDtypeStruct((b * d_b, tail_rows), F32),
                 jax.ShapeDtypeStruct((b * d_b, tail_rows), F32)]
    out_specs = [pl.BlockSpec((tile, d), lambda i: (i, 0)),
                 pl.BlockSpec((d_b, tile), tail), pl.BlockSpec((d_b, tile), tail)]
    if emit_vn:
        out_shape.append(jax.ShapeDtypeStruct((b * s, d_a), F32))
        out_specs.append(pl.BlockSpec((tile, d_a), lambda i: (i, 0)))

    return pl.pallas_call(
        functools.partial(_mixer_kernel, tile=tile, t_mlp=t_mlp, n_t=n_t, has_cache=has_cache, emit_vn=emit_vn),
        grid=(b * n_t,),
        in_specs=in_specs,
        out_specs=out_specs,
        out_shape=out_shape,
        scratch_shapes=_scratch(s, tile, d_b, n_heads),
        compiler_params=pltpu.CompilerParams(dimension_semantics=("arbitrary",), vmem_limit_bytes=VMEM_LIMIT),
        name="mixer",
    )(*args)


def _ffn(h, p, ffn_w, *, tile_m, ff_chunk):
    m, d = h.shape
    d_p = p.shape[-1]
    return pl.pallas_call(
        functools.partial(_ffn_kernel, ff_chunk=ff_chunk),
        grid=(m // tile_m,),
        in_specs=[pl.BlockSpec((tile_m, d), lambda i: (i, 0)),
                  pl.BlockSpec((tile_m, d_p), lambda i: (i, 0))] + [_WHOLE_VMEM] * N_FFN_W,
        out_specs=pl.BlockSpec((tile_m, d), lambda i: (i, 0)),
        out_shape=jax.ShapeDtypeStruct((m, d), F32),
        compiler_params=pltpu.CompilerParams(dimension_semantics=("arbitrary",), vmem_limit_bytes=VMEM_LIMIT),
        name="ffn",
    )(h, p, *ffn_w)


def _layer(x, p, mix_w, ffn_w, *, tile, t_mlp, tail_rows, ff_chunk):
    b, s, d = x.shape
    d_p = p.shape[-1]
    d_b = mix_w[8].shape[-1]
    n_heads = mix_w[6].shape[0]
    n_t = s // tile
    n_tiles = b * n_t
    tail = _tail_map(n_t, tail_rows // tile)

    def mix_tile(i):
        return jnp.minimum(i, n_tiles - 1)

    def ffn_block(i):
        return jnp.maximum(i - 1, 0), 0

    return pl.pallas_call(
        functools.partial(_layer_kernel, tile=tile, t_mlp=t_mlp, n_t=n_t, n_tiles=n_tiles, ff_chunk=ff_chunk),
        grid=(n_tiles + 1,),
        in_specs=[pl.BlockSpec((tile, d), lambda i: (mix_tile(i), 0)), pl.BlockSpec((tile, d_p), ffn_block)]
        + [_WHOLE_VMEM] * (N_MIX_W + N_FFN_W),
        out_specs=[pl.BlockSpec((tile, d), ffn_block),
                   pl.BlockSpec((d_b, tile), lambda i: tail(mix_tile(i))),
                   pl.BlockSpec((d_b, tile), lambda i: tail(mix_tile(i)))],
        out_shape=[jax.ShapeDtypeStruct((b * s, d), F32),
                   jax.ShapeDtypeStruct((b * d_b, tail_rows), F32),
                   jax.ShapeDtypeStruct((b * d_b, tail_rows), F32)],
        scratch_shapes=_scratch(s, tile, d_b, n_heads) + [pltpu.VMEM((2, tile, d), F32)],
        compiler_params=pltpu.CompilerParams(dimension_semantics=("arbitrary",), vmem_limit_bytes=VMEM_LIMIT),
        name="layer",
    )(x.reshape(b * s, d), p.reshape(b * s, d_p), *mix_w, *ffn_w)


def _pair_spatial(w_s, b_s, t):
    g = w_s.shape[0]
    wsp = w_s.reshape(g // 2, 2, t, t).transpose(0, 2, 1, 3).reshape(g // 2, t, 2 * t)
    bsp = jnp.repeat(b_s.reshape(g // 2, 2, t).transpose(0, 2, 1), GROUP, axis=-1)
    return wsp.astype(F32), bsp.astype(F32)


def kernel(x_prompt, x_sample, p_prompt, p_sample, cache_band_k, cache_band_v, g_attn, w_in, g_v, w_spatial, b_spatial, rel_bias, g_out_a, g_out_b, w_out, g_ffn, w_gate, w_up, w_down, g_ple, w_ple_gate, w_ple_proj, g_final):
    depth = w_in.shape[0]
    assert depth == 1, "the final rmsnorm is fused into the (single) layer's ffn stage"
    bp, sp, d = x_prompt.shape
    bs, ss, _ = x_sample.shape
    d_b = g_out_b.shape[-1]
    n_heads = rel_bias.shape[1]
    n_groups = g_v.shape[1]
    r_p = min(PAST, sp)
    tile_p = 256

    half = 2 * LANES
    pblk = (np.arange(half)[:, None] // GROUP == np.arange(half)[None, :] // GROUP)
    pblk = jnp.asarray(pblk, BF16)

    def row(g):
        return g.reshape(1, -1).astype(F32)

    def mixer_weights(t_mlp, tile):
        wsp, bsp = _pair_spatial(w_spatial[0][:, :t_mlp, :t_mlp], b_spatial[0][:, :t_mlp], t_mlp)
        return (row(g_attn[0]), w_in[0].astype(BF16), row(g_v[0]), pblk, wsp, bsp,
                _rel_bias_by_offset(rel_bias[0], tile), row(g_out_a[0]), row(g_out_b[0]), w_out[0].astype(BF16))

    ffn_w = (row(g_ffn[0]), w_gate[0].astype(BF16), w_up[0].astype(BF16), w_down[0].astype(BF16),
             row(g_ple[0]), w_ple_gate[0].astype(BF16), w_ple_proj[0].astype(BF16), row(g_final))

    yp, kp, vp = _layer(x_prompt, p_prompt[0], mixer_weights(MLP_CHUNK, tile_p), ffn_w,
                        tile=tile_p, t_mlp=MLP_CHUNK, tail_rows=r_p, ff_chunk=768)
    ck = cache_band_k[0].reshape(bs, PAST, d_b)
    cv = cache_band_v[0].reshape(bs, PAST, d_b)
    hs1, ks, vs, va = _mixer(x_sample, ck, cv, mixer_weights(ss, ss),
                             tile=ss, t_mlp=ss, tail_rows=ss, emit_vn=True)
    ys = _ffn(hs1, p_sample[0].reshape(bs * ss, -1), ffn_w, tile_m=bs * ss, ff_chunk=1408)

    def band_rows(tails, b, rows):
        return tails.reshape(1, b, n_heads, GROUP, rows).transpose(0, 1, 4, 2, 3)

    return (yp.reshape(bp, sp, d), ys.reshape(bs, ss, d),
            band_rows(kp, bp, r_p), band_rows(vp, bp, r_p), band_rows(ks, bs, ss), band_rows(vs, bs, ss),
            va.reshape(1, bs, ss, n_groups, GROUP))
```

```python
import functools

import numpy as np
import jax
import jax.numpy as jnp
from jax import lax
from jax.experimental import pallas as pl
from jax.experimental.pallas import tpu as pltpu

CHUNK = 64
N_BAND_CHUNKS = 8
PAST = CHUNK * N_BAND_CHUNKS
MLP_CHUNK = 128
GROUP = 64
REL_CLIP = 64
EPS = 1e-6
NEG_INF = -1e30
LANES = 128
VMEM_LIMIT = 60 * 1024 * 1024

BF16 = jnp.bfloat16
F32 = jnp.float32
_SQRT_2_OVER_PI = float(np.sqrt(2.0 / np.pi).astype(np.float32))
LOG2_E = float(np.log2(np.e))
_WHOLE_VMEM = pl.BlockSpec(memory_space=pltpu.VMEM)
N_MIX_W = 10
N_FFN_W = 8


def _inv_rms(x):
    return lax.rsqrt(jnp.mean(x * x, axis=-1, keepdims=True) + EPS)


def _rms(x, g):
    return x * _inv_rms(x) * g


def _rms_dot(x, g, w):
    return _dot((x * g).astype(BF16), w) * _inv_rms(x)


def _gelu(x):
    half_x = 0.5 * x
    inner = x * (_SQRT_2_OVER_PI + (_SQRT_2_OVER_PI * 0.044715) * (x * x))
    return half_x + half_x * jnp.tanh(inner)


def _dot(a, b):
    return jnp.dot(a, b, preferred_element_type=F32)


def _dot_nt(a, b):
    return lax.dot_general(a, b, (((1,), (1,)), ((), ())), preferred_element_type=F32)


def _even_head_lanes(shape):
    return (lax.broadcasted_iota(jnp.int32, shape, 1) & GROUP) == 0


def _split_heads(x):
    even = _even_head_lanes(x.shape)
    zero = jnp.zeros_like(x)
    return jnp.where(even, x, zero).astype(BF16), jnp.where(even, zero, x).astype(BF16)


def _init_bias(relb_ref, bias_scr, tile):
    n = relb_ref.shape[-1]
    window = PAST + tile
    rc = lax.broadcasted_iota(jnp.int32, (tile, n), 0) >> 6
    cc = lax.broadcasted_iota(jnp.int32, (tile, n), 1) >> 6
    band = (cc >= rc) & (cc <= rc + N_BAND_CHUNKS)
    for h in range(relb_ref.shape[0]):
        base = jnp.broadcast_to(relb_ref[h:h + 1, :] * LOG2_E, (tile, n))
        toep = pltpu.roll(base, 0, 1, stride=1, stride_axis=0)
        bias_scr[h // 2, (h % 2) * tile:(h % 2 + 1) * tile, :] = jnp.where(band, toep, NEG_INF)[:, 0:window]


def _mixer_tile(x, t, w_refs, k_scr, v_scr, bias_scr, kt_ref, vt_ref, vn_ref, *, tile, t_mlp, mask_pad,
                after_scores=lambda: None):
    g_attn_ref, w_in_ref, g_v_ref, pblk_ref, wsp_ref, bsp_ref, _, g_oa_ref, g_ob_ref, w_out_ref = w_refs
    d_a, d_b = g_oa_ref.shape[-1], g_ob_ref.shape[-1]
    window = PAST + tile
    row0 = pl.multiple_of(t * tile, tile)

    z = _rms_dot(x, g_attn_ref[...], w_in_ref[...])
    u = _gelu(z[:, 0:d_a])
    v = _gelu(z[:, d_a:2 * d_a])
    q = z[:, 2 * d_a:2 * d_a + d_b] * (GROUP ** -0.5 * LOG2_E)
    k = z[:, 2 * d_a + d_b:2 * d_a + 2 * d_b]
    vb = z[:, 2 * d_a + 2 * d_b:2 * d_a + 3 * d_b]

    k_scr[pl.ds(PAST + row0, tile), :] = k.astype(BF16)
    v_scr[pl.ds(PAST + row0, tile), :] = vb.astype(BF16)
    kt_ref[...] = k.T
    vt_ref[...] = vb.T

    v2 = (v * v).astype(BF16)
    half = pblk_ref.shape[0]
    ms = jnp.concatenate([_dot(v2[:, c:c + half], pblk_ref[...]) for c in range(0, d_a, half)],
                         axis=1) * (1.0 / GROUP)
    vn = v * lax.rsqrt(ms + EPS) * g_v_ref[...]
    if vn_ref is not None:
        vn_ref[...] = vn

    vn_halves = _split_heads(vn)
    nblk = tile // t_mlp
    wrow = lax.broadcasted_iota(jnp.int32, (t_mlp, 2 * t_mlp), 0)
    wcol = lax.broadcasted_iota(jnp.int32, (t_mlp, 2 * t_mlp), 1)
    tril = jnp.where(wcol >= t_mlp, wcol - t_mlp, wcol) <= wrow
    ya_slabs = []
    for j in range(d_a // LANES):
        w = wsp_ref[j]
        w = jnp.where(tril, w, jnp.zeros_like(w)).astype(BF16)
        cols = [jnp.concatenate([vh[blk * t_mlp:(blk + 1) * t_mlp, j * LANES:(j + 1) * LANES]
                                 for vh in vn_halves], axis=0) for blk in range(nblk)]
        rhs = cols[0] if nblk == 1 else jnp.concatenate(cols, axis=1)
        mixed = _dot(w, rhs)
        parts = []
        for blk in range(nblk):
            ub = u[blk * t_mlp:(blk + 1) * t_mlp, j * LANES:(j + 1) * LANES]
            parts.append(ub * (mixed[:, blk * LANES:(blk + 1) * LANES] + bsp_ref[j]))
        ya_slabs.append(parts[0] if nblk == 1 else jnp.concatenate(parts, axis=0))
    ya = jnp.concatenate(ya_slabs, axis=1)

    q_halves = _split_heads(q)
    even_lane = _even_head_lanes((tile, LANES))
    if mask_pad:
        colw = lax.broadcasted_iota(jnp.int32, (1, PAST), 1)
        pen = jnp.where(colw >= PAST - row0, 0.0, NEG_INF).astype(F32)
    yb_slabs = []
    for j in range(d_b // LANES):
        kw = k_scr[pl.ds(row0, window), j * LANES:(j + 1) * LANES]
        vw = v_scr[pl.ds(row0, window), j * LANES:(j + 1) * LANES]
        qp = jnp.concatenate([qh[:, j * LANES:(j + 1) * LANES] for qh in q_halves], axis=0)
        s = _dot_nt(qp, kw)
        after_scores()
        s = s + bias_scr[j]
        if mask_pad:
            s = jnp.concatenate([s[:, 0:PAST] + pen, s[:, PAST:window]], axis=1)
        m = jnp.max(s, axis=-1, keepdims=True)
        p = jnp.exp2(s - m).astype(BF16)
        ol = _dot(p, jnp.concatenate([vw, jnp.ones_like(vw)], axis=1))
        o = ol[:, 0:LANES] * (1.0 / ol[:, LANES:2 * LANES])
        yb_slabs.append(jnp.where(even_lane, o[0:tile], o[tile:2 * tile]))
    yb = jnp.concatenate(yb_slabs, axis=1)

    return (x + _rms_dot(ya, g_oa_ref[...], w_out_ref[0:d_a, :])
            + _rms_dot(yb, g_ob_ref[...], w_out_ref[d_a:d_a + d_b, :]))


def _ffn_stages(h, p, w_refs, *, ff_chunk):
    g_ffn_ref, wg_ref, wu_ref, wd_ref, g_ple_ref, wpg_ref, wpp_ref, g_fin_ref = w_refs
    f = (h * g_ffn_ref[...]).astype(BF16)
    r = _inv_rms(h)
    d_ff = wg_ref.shape[1]
    acc = h
    act_prev = None
    for c0 in range(0, d_ff, ff_chunk):
        c1 = min(c0 + ff_chunk, d_ff)
        gate = _dot(f, wg_ref[:, c0:c1]) * r
        up = _dot(f, wu_ref[:, c0:c1]) * r
        if act_prev is not None:
            acc = acc + _dot(act_prev, wd_ref[c0 - ff_chunk:c0, :])
        act_prev = (gate * jax.nn.sigmoid(gate) * up).astype(BF16)
        yield
    acc = acc + _dot(act_prev, wd_ref[c0:d_ff, :])
    pgate = jax.nn.sigmoid(_rms_dot(acc, g_ple_ref[...], wpg_ref[...]))
    proj = _dot(p.astype(BF16), wpp_ref[...])
    return _rms(acc + pgate * proj, g_fin_ref[...])


def _advance(stages):
    try:
        next(stages)
    except StopIteration as done:
        return done.value
    return None


def _ffn_tile(h, p, w_refs, *, ff_chunk):
    stages = _ffn_stages(h, p, w_refs, ff_chunk=ff_chunk)
    while True:
        y = _advance(stages)
        if y is not None:
            return y


def _fill_past(k_scr, v_scr, ck_ref, cv_ref):
    d_b = k_scr.shape[-1]
    if ck_ref is None:
        k_scr[0:PAST, :] = jnp.zeros((PAST, d_b), BF16)
        v_scr[0:PAST, :] = jnp.zeros((PAST, d_b), BF16)
    else:
        k_scr[0:PAST, :] = ck_ref[0].astype(BF16)
        v_scr[0:PAST, :] = cv_ref[0].astype(BF16)


def _mixer_kernel(*refs, tile, t_mlp, n_t, has_cache, emit_vn):
    it = iter(refs)
    x_ref = next(it)
    ck_ref = next(it) if has_cache else None
    cv_ref = next(it) if has_cache else None
    w_refs = tuple(next(it) for _ in range(N_MIX_W))
    h_ref, kt_ref, vt_ref = (next(it) for _ in range(3))
    vn_ref = next(it) if emit_vn else None
    k_scr, v_scr, bias_scr = (next(it) for _ in range(3))

    i = pl.program_id(0)
    t = lax.rem(i, n_t)

    @pl.when(i == 0)
    def _():
        _init_bias(w_refs[6], bias_scr, tile)

    @pl.when(t == 0)
    def _():
        _fill_past(k_scr, v_scr, ck_ref, cv_ref)

    h_ref[...] = _mixer_tile(x_ref[...], t, w_refs, k_scr, v_scr, bias_scr, kt_ref, vt_ref, vn_ref,
                             tile=tile, t_mlp=t_mlp, mask_pad=not has_cache)


def _ffn_kernel(h_ref, p_ref, *refs, ff_chunk):
    y_ref = refs[N_FFN_W]
    y_ref[...] = _ffn_tile(h_ref[...], p_ref[...], refs[:N_FFN_W], ff_chunk=ff_chunk)


def _layer_kernel(x_ref, p_ref, *refs, tile, t_mlp, n_t, n_tiles, ff_chunk):
    mix_w = refs[:N_MIX_W]
    ffn_w = refs[N_MIX_W:N_MIX_W + N_FFN_W]
    y_ref, kt_ref, vt_ref, k_scr, v_scr, bias_scr, h_scr = refs[N_MIX_W + N_FFN_W:]

    i = pl.program_id(0)
    t = lax.rem(jnp.minimum(i, n_tiles - 1), n_t)

    @pl.when(i == 0)
    def _():
        _init_bias(mix_w[6], bias_scr, tile)
        h_scr[1] = jnp.zeros(h_scr.shape[1:], F32)

    @pl.when(t == 0)
    def _():
        _fill_past(k_scr, v_scr, None, None)

    h_prev = h_scr[lax.rem(i + 1, 2)]
    stages = _ffn_stages(h_prev, p_ref[...], ffn_w, ff_chunk=ff_chunk)
    h_scr[lax.rem(i, 2)] = _mixer_tile(x_ref[...], t, mix_w, k_scr, v_scr, bias_scr, kt_ref, vt_ref, None,
                                       tile=tile, t_mlp=t_mlp, mask_pad=True,
                                       after_scores=lambda: _advance(stages))
    y = None
    while y is None:
        y = _advance(stages)
    y_ref[...] = y


def _rel_bias_by_offset(rel_bias, tile):
    n = PAST + 2 * tile
    e = np.arange(n)
    e = np.where(e < PAST + tile, e, e - n)
    idx = np.clip(PAST - e, -REL_CLIP, REL_CLIP) + REL_CLIP
    return rel_bias[:, idx].astype(F32)


def _tail_map(n_t, tail_tiles):
    first_tail = n_t - tail_tiles

    def index(i):
        return i // n_t, jnp.maximum(lax.rem(i, n_t) - first_tail, 0)
    return index


def _scratch(s, tile, d_b, n_heads):
    return [pltpu.VMEM((PAST + s, d_b), BF16), pltpu.VMEM((PAST + s, d_b), BF16),
            pltpu.VMEM((n_heads // 2, 2 * tile, PAST + tile), F32)]


def _mixer(x, cache_k, cache_v, mix_w, *, tile, t_mlp, tail_rows, emit_vn):
    b, s, d = x.shape
    d_a, d_b = mix_w[7].shape[-1], mix_w[8].shape[-1]
    n_heads = mix_w[6].shape[0]
    has_cache = cache_k is not None
    n_t = s // tile
    tail = _tail_map(n_t, tail_rows // tile)

    in_specs = [pl.BlockSpec((tile, d), lambda i: (i, 0))]
    args = [x.reshape(b * s, d)]
    if has_cache:
        in_specs += [pl.BlockSpec((1, PAST, d_b), lambda i: (i // n_t, 0, 0))] * 2
        args += [cache_k, cache_v]
    in_specs += [_WHOLE_VMEM] * N_MIX_W
    args += list(mix_w)

    out_shape = [jax.ShapeDtypeStruct((b * s, d), F32),
                 jax.ShapeDtypeStruct((b * d_b, tail_rows), F32),
                 jax.ShapeDtypeStruct((b * d_b, tail_rows), F32)]
    out_specs = [pl.BlockSpec((tile, d), lambda i: (i, 0)),
                 pl.BlockSpec((d_b, tile), tail), pl.BlockSpec((d_b, tile), tail)]
    if emit_vn:
        out_shape.append(jax.ShapeDtypeStruct((b * s, d_a), F32))
        out_specs.append(pl.BlockSpec((tile, d_a), lambda i: (i, 0)))

    return pl.pallas_call(
        functools.partial(_mixer_kernel, tile=tile, t_mlp=t_mlp, n_t=n_t, has_cache=has_cache, emit_vn=emit_vn),
        grid=(b * n_t,),
        in_specs=in_specs,
        out_specs=out_specs,
        out_shape=out_shape,
        scratch_shapes=_scratch(s, tile, d_b, n_heads),
        compiler_params=pltpu.CompilerParams(dimension_semantics=("arbitrary",), vmem_limit_bytes=VMEM_LIMIT),
        name="mixer",
    )(*args)


def _ffn(h, p, ffn_w, *, tile_m, ff_chunk):
    m, d = h.shape
    d_p = p.shape[-1]
    return pl.pallas_call(
        functools.partial(_ffn_kernel, ff_chunk=ff_chunk),
        grid=(m // tile_m,),
        in_specs=[pl.BlockSpec((tile_m, d), lambda i: (i, 0)),
                  pl.BlockSpec((tile_m, d_p), lambda i: (i, 0))] + [_WHOLE_VMEM] * N_FFN_W,
        out_specs=pl.BlockSpec((tile_m, d), lambda i: (i, 0)),
        out_shape=jax.ShapeDtypeStruct((m, d), F32),
        compiler_params=pltpu.CompilerParams(dimension_semantics=("arbitrary",), vmem_limit_bytes=VMEM_LIMIT),
        name="ffn",
    )(h, p, *ffn_w)


def _layer(x, p, mix_w, ffn_w, *, tile, t_mlp, tail_rows, ff_chunk):
    b, s, d = x.shape
    d_p = p.shape[-1]
    d_b = mix_w[8].shape[-1]
    n_heads = mix_w[6].shape[0]
    n_t = s // tile
    n_tiles = b * n_t
    tail = _tail_map(n_t, tail_rows // tile)

    def mix_tile(i):
        return jnp.minimum(i, n_tiles - 1)

    def ffn_block(i):
        return jnp.maximum(i - 1, 0), 0

    return pl.pallas_call(
        functools.partial(_layer_kernel, tile=tile, t_mlp=t_mlp, n_t=n_t, n_tiles=n_tiles, ff_chunk=ff_chunk),
        grid=(n_tiles + 1,),
        in_specs=[pl.BlockSpec((tile, d), lambda i: (mix_tile(i), 0)), pl.BlockSpec((tile, d_p), ffn_block)]
        + [_WHOLE_VMEM] * (N_MIX_W + N_FFN_W),
        out_specs=[pl.BlockSpec((tile, d), ffn_block),
                   pl.BlockSpec((d_b, tile), lambda i: tail(mix_tile(i))),
                   pl.BlockSpec((d_b, tile), lambda i: tail(mix_tile(i)))],
        out_shape=[jax.ShapeDtypeStruct((b * s, d), F32),
                   jax.ShapeDtypeStruct((b * d_b, tail_rows), F32),
                   jax.ShapeDtypeStruct((b * d_b, tail_rows), F32)],
        scratch_shapes=_scratch(s, tile, d_b, n_heads) + [pltpu.VMEM((2, tile, d), F32)],
        compiler_params=pltpu.CompilerParams(dimension_semantics=("arbitrary",), vmem_limit_bytes=VMEM_LIMIT),
        name="layer",
    )(x.reshape(b * s, d), p.reshape(b * s, d_p), *mix_w, *ffn_w)


def _pair_spatial(w_s, b_s, t):
    g = w_s.shape[0]
    wsp = w_s.reshape(g // 2, 2, t, t).transpose(0, 2, 1, 3).reshape(g // 2, t, 2 * t)
    bsp = jnp.repeat(b_s.reshape(g // 2, 2, t).transpose(0, 2, 1), GROUP, axis=-1)
    return wsp.astype(F32), bsp.astype(F32)


def kernel(x_prompt, x_sample, p_prompt, p_sample, cache_band_k, cache_band_v, g_attn, w_in, g_v, w_spatial, b_spatial, rel_bias, g_out_a, g_out_b, w_out, g_ffn, w_gate, w_up, w_down, g_ple, w_ple_gate, w_ple_proj, g_final):
    depth = w_in.shape[0]
    assert depth == 1, "the final rmsnorm is fused into the (single) layer's ffn stage"
    bp, sp, d = x_prompt.shape
    bs, ss, _ = x_sample.shape
    d_b = g_out_b.shape[-1]
    n_heads = rel_bias.shape[1]
    n_groups = g_v.shape[1]
    r_p = min(PAST, sp)
    tile_p = 256

    half = 2 * LANES
    pblk = (np.arange(half)[:, None] // GROUP == np.arange(half)[None, :] // GROUP)
    pblk = jnp.asarray(pblk, BF16)

    def row(g):
        return g.reshape(1, -1).astype(F32)

    def mixer_weights(t_mlp, tile):
        wsp, bsp = _pair_spatial(w_spatial[0][:, :t_mlp, :t_mlp], b_spatial[0][:, :t_mlp], t_mlp)
        return (row(g_attn[0]), w_in[0].astype(BF16), row(g_v[0]), pblk, wsp, bsp,
                _rel_bias_by_offset(rel_bias[0], tile), row(g_out_a[0]), row(g_out_b[0]), w_out[0].astype(BF16))

    ffn_w = (row(g_ffn[0]), w_gate[0].astype(BF16), w_up[0].astype(BF16), w_down[0].astype(BF16),
             row(g_ple[0]), w_ple_gate[0].astype(BF16), w_ple_proj[0].astype(BF16), row(g_final))

    yp, kp, vp = _layer(x_prompt, p_prompt[0], mixer_weights(MLP_CHUNK, tile_p), ffn_w,
                        tile=tile_p, t_mlp=MLP_CHUNK, tail_rows=r_p, ff_chunk=768)
    ck = cache_band_k[0].reshape(bs, PAST, d_b)
    cv = cache_band_v[0].reshape(bs, PAST, d_b)
    hs1, ks, vs, va = _mixer(x_sample, ck, cv, mixer_weights(ss, ss),
                             tile=ss, t_mlp=ss, tail_rows=ss, emit_vn=True)
    ys = _ffn(hs1, p_sample[0].reshape(bs * ss, -1), ffn_w, tile_m=bs * ss, ff_chunk=1408)

    def band_rows(tails, b, rows):
        return tails.reshape(1, b, n_heads, GROUP, rows).transpose(0, 1, 4, 2, 3)

    return (yp.reshape(bp, sp, d), ys.reshape(bs, ss, d),
            band_rows(kp, bp, r_p), band_rows(vp, bp, r_p), band_rows(ks, bs, ss), band_rows(vs, bs, ss),
            va.reshape(1, bs, ss, n_groups, GROUP))
```

```python
import functools

import numpy as np
import jax
import jax.numpy as jnp
from jax import lax
from jax.experimental import pallas as pl
from jax.experimental.pallas import tpu as pltpu

CHUNK = 64
N_BAND_CHUNKS = 8
PAST = CHUNK * N_BAND_CHUNKS
MLP_CHUNK = 128
GROUP = 64
REL_CLIP = 64
EPS = 1e-6
NEG_INF = -1e30
LANES = 128
VMEM_LIMIT = 60 * 1024 * 1024

BF16 = jnp.bfloat16
F32 = jnp.float32
_SQRT_2_OVER_PI = float(np.sqrt(2.0 / np.pi).astype(np.float32))
LOG2_E = float(np.log2(np.e))
_WHOLE_VMEM = pl.BlockSpec(memory_space=pltpu.VMEM)
N_MIX_W = 10
N_FFN_W = 8


def _inv_rms(x):
    return lax.rsqrt(jnp.mean(x * x, axis=-1, keepdims=True) + EPS)


def _rms(x, g):
    return x * _inv_rms(x) * g


def _rms_dot(x, g, w):
    return _dot((x * g).astype(BF16), w) * _inv_rms(x)


def _gelu(x):
    half_x = 0.5 * x
    inner = x * (_SQRT_2_OVER_PI + (_SQRT_2_OVER_PI * 0.044715) * (x * x))
    return half_x + half_x * jnp.tanh(inner)


def _dot(a, b):
    return jnp.dot(a, b, preferred_element_type=F32)


def _dot_nt(a, b):
    return lax.dot_general(a, b, (((1,), (1,)), ((), ())), preferred_element_type=F32)


def _even_head_lanes(shape):
    return (lax.broadcasted_iota(jnp.int32, shape, 1) & GROUP) == 0


def _split_heads(x):
    even = _even_head_lanes(x.shape)
    zero = jnp.zeros_like(x)
    return jnp.where(even, x, zero).astype(BF16), jnp.where(even, zero, x).astype(BF16)


def _init_bias(relb_ref, bias_scr, tile):
    n = relb_ref.shape[-1]
    window = PAST + tile
    rc = lax.broadcasted_iota(jnp.int32, (tile, n), 0) >> 6
    cc = lax.broadcasted_iota(jnp.int32, (tile, n), 1) >> 6
    band = (cc >= rc) & (cc <= rc + N_BAND_CHUNKS)
    for h in range(relb_ref.shape[0]):
        base = jnp.broadcast_to(relb_ref[h:h + 1, :] * LOG2_E, (tile, n))
        toep = pltpu.roll(base, 0, 1, stride=1, stride_axis=0)
        bias_scr[h // 2, (h % 2) * tile:(h % 2 + 1) * tile, :] = jnp.where(band, toep, NEG_INF)[:, 0:window]


def _mixer_tile(x, t, w_refs, k_scr, v_scr, bias_scr, kt_ref, vt_ref, vn_ref, *, tile, t_mlp, mask_pad,
                after_scores=lambda: None):
    g_attn_ref, w_in_ref, g_v_ref, pblk_ref, wsp_ref, bsp_ref, _, g_oa_ref, g_ob_ref, w_out_ref = w_refs
    d_a, d_b = g_oa_ref.shape[-1], g_ob_ref.shape[-1]
    window = PAST + tile
    row0 = pl.multiple_of(t * tile, tile)

    z = _rms_dot(x, g_attn_ref[...], w_in_ref[...])
    u = _gelu(z[:, 0:d_a])
    v = _gelu(z[:, d_a:2 * d_a])
    q = z[:, 2 * d_a:2 * d_a + d_b] * (GROUP ** -0.5 * LOG2_E)
    k = z[:, 2 * d_a + d_b:2 * d_a + 2 * d_b]
    vb = z[:, 2 * d_a + 2 * d_b:2 * d_a + 3 * d_b]

    k_scr[pl.ds(PAST + row0, tile), :] = k.astype(BF16)
    v_scr[pl.ds(PAST + row0, tile), :] = vb.astype(BF16)
    kt_ref[...] = k.T
    vt_ref[...] = vb.T

    v2 = (v * v).astype(BF16)
    half = pblk_ref.shape[0]
    ms = jnp.concatenate([_dot(v2[:, c:c + half], pblk_ref[...]) for c in range(0, d_a, half)],
                         axis=1) * (1.0 / GROUP)
    vn = v * lax.rsqrt(ms + EPS) * g_v_ref[...]
    if vn_ref is not None:
        vn_ref[...] = vn

    vn_halves = _split_heads(vn)
    nblk = tile // t_mlp
    wrow = lax.broadcasted_iota(jnp.int32, (t_mlp, 2 * t_mlp), 0)
    wcol = lax.broadcasted_iota(jnp.int32, (t_mlp, 2 * t_mlp), 1)
    tril = jnp.where(wcol >= t_mlp, wcol - t_mlp, wcol) <= wrow
    ya_slabs = []
    for j in range(d_a // LANES):
        w = wsp_ref[j]
        w = jnp.where(tril, w, jnp.zeros_like(w)).astype(BF16)
        cols = [jnp.concatenate([vh[blk * t_mlp:(blk + 1) * t_mlp, j * LANES:(j + 1) * LANES]
                                 for vh in vn_halves], axis=0) for blk in range(nblk)]
        rhs = cols[0] if nblk == 1 else jnp.concatenate(cols, axis=1)
        mixed = _dot(w, rhs)
        parts = []
        for blk in range(nblk):
            ub = u[blk * t_mlp:(blk + 1) * t_mlp, j * LANES:(j + 1) * LANES]
            parts.append(ub * (mixed[:, blk * LANES:(blk + 1) * LANES] + bsp_ref[j]))
        ya_slabs.append(parts[0] if nblk == 1 else jnp.concatenate(parts, axis=0))
    ya = jnp.concatenate(ya_slabs, axis=1)

    q_halves = _split_heads(q)
    even_lane = _even_head_lanes((tile, LANES))
    if mask_pad:
        colw = lax.broadcasted_iota(jnp.int32, (1, PAST), 1)
        pen = jnp.where(colw >= PAST - row0, 0.0, NEG_INF).astype(F32)
    yb_slabs = []
    for j in range(d_b // LANES):
        kw = k_scr[pl.ds(row0, window), j * LANES:(j + 1) * LANES]
        vw = v_scr[pl.ds(row0, window), j * LANES:(j + 1) * LANES]
        qp = jnp.concatenate([qh[:, j * LANES:(j + 1) * LANES] for qh in q_halves], axis=0)
        s = _dot_nt(qp, kw)
        after_scores()
        s = s + bias_scr[j]
        if mask_pad:
            s = jnp.concatenate([s[:, 0:PAST] + pen, s[:, PAST:window]], axis=1)
        m = jnp.max(s, axis=-1, keepdims=True)
        p = jnp.exp2(s - m).astype(BF16)
        ol = _dot(p, jnp.concatenate([vw, jnp.ones_like(vw)], axis=1))
        o = ol[:, 0:LANES] * (1.0 / ol[:, LANES:2 * LANES])
        yb_slabs.append(jnp.where(even_lane, o[0:tile], o[tile:2 * tile]))
    yb = jnp.concatenate(yb_slabs, axis=1)

    return (x + _rms_dot(ya, g_oa_ref[...], w_out_ref[0:d_a, :])
            + _rms_dot(yb, g_ob_ref[...], w_out_ref[d_a:d_a + d_b, :]))


def _ffn_stages(h, p, w_refs, *, ff_chunk):
    g_ffn_ref, wg_ref, wu_ref, wd_ref, g_ple_ref, wpg_ref, wpp_ref, g_fin_ref = w_refs
    f = (h * g_ffn_ref[...]).astype(BF16)
    r = _inv_rms(h)
    d_ff = wg_ref.shape[1]
    acc = h
    act_prev = None
    for c0 in range(0, d_ff, ff_chunk):
        c1 = min(c0 + ff_chunk, d_ff)
        gate = _dot(f, wg_ref[:, c0:c1]) * r
        up = _dot(f, wu_ref[:, c0:c1]) * r
        if act_prev is not None:
            acc = acc + _dot(act_prev, wd_ref[c0 - ff_chunk:c0, :])
        act_prev = (gate * jax.nn.sigmoid(gate) * up).astype(BF16)
        yield
    acc = acc + _dot(act_prev, wd_ref[c0:d_ff, :])
    pgate = jax.nn.sigmoid(_rms_dot(acc, g_ple_ref[...], wpg_ref[...]))
    proj = _dot(p.astype(BF16), wpp_ref[...])
    return _rms(acc + pgate * proj, g_fin_ref[...])


def _advance(stages):
    try:
        next(stages)
    except StopIteration as done:
        return done.value
    return None


def _ffn_tile(h, p, w_refs, *, ff_chunk):
    stages = _ffn_stages(h, p, w_refs, ff_chunk=ff_chunk)
    while True:
        y = _advance(stages)
        if y is not None:
            return y


def _fill_past(k_scr, v_scr, ck_ref, cv_ref):
    d_b = k_scr.shape[-1]
    if ck_ref is None:
        k_scr[0:PAST, :] = jnp.zeros((PAST, d_b), BF16)
        v_scr[0:PAST, :] = jnp.zeros((PAST, d_b), BF16)
    else:
        k_scr[0:PAST, :] = ck_ref[0].astype(BF16)
        v_scr[0:PAST, :] = cv_ref[0].astype(BF16)


def _mixer_kernel(*refs, tile, t_mlp, n_t, has_cache, emit_vn):
    it = iter(refs)
    x_ref = next(it)
    ck_ref = next(it) if has_cache else None
    cv_ref = next(it) if has_cache else None
    w_refs = tuple(next(it) for _ in range(N_MIX_W))
    h_ref, kt_ref, vt_ref = (next(it) for _ in range(3))
    vn_ref = next(it) if emit_vn else None
    k_scr, v_scr, bias_scr = (next(it) for _ in range(3))

    i = pl.program_id(0)
    t = lax.rem(i, n_t)

    @pl.when(i == 0)
    def _():
        _init_bias(w_refs[6], bias_scr, tile)

    @pl.when(t == 0)
    def _():
        _fill_past(k_scr, v_scr, ck_ref, cv_ref)

    h_ref[...] = _mixer_tile(x_ref[...], t, w_refs, k_scr, v_scr, bias_scr, kt_ref, vt_ref, vn_ref,
                             tile=tile, t_mlp=t_mlp, mask_pad=not has_cache)


def _ffn_kernel(h_ref, p_ref, *refs, ff_chunk):
    y_ref = refs[N_FFN_W]
    y_ref[...] = _ffn_tile(h_ref[...], p_ref[...], refs[:N_FFN_W], ff_chunk=ff_chunk)


def _layer_kernel(x_ref, p_ref, *refs, tile, t_mlp, n_t, n_tiles, ff_chunk):
    mix_w = refs[:N_MIX_W]
    ffn_w = refs[N_MIX_W:N_MIX_W + N_FFN_W]
    y_ref, kt_ref, vt_ref, k_scr, v_scr, bias_scr, h_scr = refs[N_MIX_W + N_FFN_W:]

    i = pl.program_id(0)
    t = lax.rem(jnp.minimum(i, n_tiles - 1), n_t)

    @pl.when(i == 0)
    def _():
        _init_bias(mix_w[6], bias_scr, tile)
        h_scr[1] = jnp.zeros(h_scr.shape[1:], F32)

    @pl.when(t == 0)
    def _():
        _fill_past(k_scr, v_scr, None, None)

    h_prev = h_scr[lax.rem(i + 1, 2)]
    stages = _ffn_stages(h_prev, p_ref[...], ffn_w, ff_chunk=ff_chunk)
    h_scr[lax.rem(i, 2)] = _mixer_tile(x_ref[...], t, mix_w, k_scr, v_scr, bias_scr, kt_ref, vt_ref, None,
                                       tile=tile, t_mlp=t_mlp, mask_pad=True,
                                       after_scores=lambda: None)
    y = None
    while y is None:
        y = _advance(stages)
    y_ref[...] = y


def _rel_bias_by_offset(rel_bias, tile):
    n = PAST + 2 * tile
    e = np.arange(n)
    e = np.where(e < PAST + tile, e, e - n)
    idx = np.clip(PAST - e, -REL_CLIP, REL_CLIP) + REL_CLIP
    return rel_bias[:, idx].astype(F32)


def _tail_map(n_t, tail_tiles):
    first_tail = n_t - tail_tiles

    def index(i):
        return i // n_t, jnp.maximum(lax.rem(i, n_t) - first_tail, 0)
    return index


def _scratch(s, tile, d_b, n_heads):
    return [pltpu.VMEM((PAST + s, d_b), BF16), pltpu.VMEM((PAST + s, d_b), BF16),
            pltpu.VMEM((n_heads // 2, 2 * tile, PAST + tile), F32)]


def _mixer(x, cache_k, cache_v, mix_w, *, tile, t_mlp, tail_rows, emit_vn):
    b, s, d = x.shape
    d_a, d_b = mix_w[7].shape[-1], mix_w[8].shape[-1]
    n_heads = mix_w[6].shape[0]
    has_cache = cache_k is not None
    n_t = s // tile
    tail = _tail_map(n_t, tail_rows // tile)

    in_specs = [pl.BlockSpec((tile, d), lambda i: (i, 0))]
    args = [x.reshape(b * s, d)]
    if has_cache:
        in_specs += [pl.BlockSpec((1, PAST, d_b), lambda i: (i // n_t, 0, 0))] * 2
        args += [cache_k, cache_v]
    in_specs += [_WHOLE_VMEM] * N_MIX_W
    args += list(mix_w)

    out_shape = [jax.ShapeDtypeStruct((b * s, d), F32),
                 jax.ShapeDtypeStruct((b * d_b, tail_rows), F32),
                 jax.ShapeDtypeStruct((b * d_b, tail_rows), F32)]
    out_specs = [pl.BlockSpec((tile, d), lambda i: (i, 0)),
                 pl.BlockSpec((d_b, tile), tail), pl.BlockSpec((d_b, tile), tail)]
    if emit_vn:
        out_shape.append(jax.ShapeDtypeStruct((b * s, d_a), F32))
        out_specs.append(pl.BlockSpec((tile, d_a), lambda i: (i, 0)))

    return pl.pallas_call(
        functools.partial(_mixer_kernel, tile=tile, t_mlp=t_mlp, n_t=n_t, has_cache=has_cache, emit_vn=emit_vn),
        grid=(b * n_t,),
        in_specs=in_specs,
        out_specs=out_specs,
        out_shape=out_shape,
        scratch_shapes=_scratch(s, tile, d_b, n_heads),
        compiler_params=pltpu.CompilerParams(dimension_semantics=("arbitrary",), vmem_limit_bytes=VMEM_LIMIT),
        name="mixer",
    )(*args)


def _ffn(h, p, ffn_w, *, tile_m, ff_chunk):
    m, d = h.shape
    d_p = p.shape[-1]
    return pl.pallas_call(
        functools.partial(_ffn_kernel, ff_chunk=ff_chunk),
        grid=(m // tile_m,),
        in_specs=[pl.BlockSpec((tile_m, d), lambda i: (i, 0)),
                  pl.BlockSpec((tile_m, d_p), lambda i: (i, 0))] + [_WHOLE_VMEM] * N_FFN_W,
        out_specs=pl.BlockSpec((tile_m, d), lambda i: (i, 0)),
        out_shape=jax.ShapeDtypeStruct((m, d), F32),
        compiler_params=pltpu.CompilerParams(dimension_semantics=("arbitrary",), vmem_limit_bytes=VMEM_LIMIT),
        name="ffn",
    )(h, p, *ffn_w)


def _layer(x, p, mix_w, ffn_w, *, tile, t_mlp, tail_rows, ff_chunk):
    b, s, d = x.shape
    d_p = p.shape[-1]
    d_b = mix_w[8].shape[-1]
    n_heads = mix_w[6].shape[0]
    n_t = s // tile
    n_tiles = b * n_t
    tail = _tail_map(n_t, tail_rows // tile)

    def mix_tile(i):
        return jnp.minimum(i, n_tiles - 1)

    def ffn_block(i):
        return jnp.maximum(i - 1, 0), 0

    return pl.pallas_call(
        functools.partial(_layer_kernel, tile=tile, t_mlp=t_mlp, n_t=n_t, n_tiles=n_tiles, ff_chunk=ff_chunk),
        grid=(n_tiles + 1,),
        in_specs=[pl.BlockSpec((tile, d), lambda i: (mix_tile(i), 0)), pl.BlockSpec((tile, d_p), ffn_block)]
        + [_WHOLE_VMEM] * (N_MIX_W + N_FFN_W),
        out_specs=[pl.BlockSpec((tile, d), ffn_block),
                   pl.BlockSpec((d_b, tile), lambda i: tail(mix_tile(i))),
                   pl.BlockSpec((d_b, tile), lambda i: tail(mix_tile(i)))],
        out_shape=[jax.ShapeDtypeStruct((b * s, d), F32),
                   jax.ShapeDtypeStruct((b * d_b, tail_rows), F32),
                   jax.ShapeDtypeStruct((b * d_b, tail_rows), F32)],
        scratch_shapes=_scratch(s, tile, d_b, n_heads) + [pltpu.VMEM((2, tile, d), F32)],
        compiler_params=pltpu.CompilerParams(dimension_semantics=("arbitrary",), vmem_limit_bytes=VMEM_LIMIT),
        name="layer",
    )(x.reshape(b * s, d), p.reshape(b * s, d_p), *mix_w, *ffn_w)


def _pair_spatial(w_s, b_s, t):
    g = w_s.shape[0]
    wsp = w_s.reshape(g // 2, 2, t, t).transpose(0, 2, 1, 3).reshape(g // 2, t, 2 * t)
    bsp = jnp.repeat(b_s.reshape(g // 2, 2, t).transpose(0, 2, 1), GROUP, axis=-1)
    return wsp.astype(F32), bsp.astype(F32)


def kernel(x_prompt, x_sample, p_prompt, p_sample, cache_band_k, cache_band_v, g_attn, w_in, g_v, w_spatial, b_spatial, rel_bias, g_out_a, g_out_b, w_out, g_ffn, w_gate, w_up, w_down, g_ple, w_ple_gate, w_ple_proj, g_final):
    depth = w_in.shape[0]
    assert depth == 1, "the final rmsnorm is fused into the (single) layer's ffn stage"
    bp, sp, d = x_prompt.shape
    bs, ss, _ = x_sample.shape
    d_b = g_out_b.shape[-1]
    n_heads = rel_bias.shape[1]
    n_groups = g_v.shape[1]
    r_p = min(PAST, sp)
    tile_p = 256

    half = 2 * LANES
    pblk = (np.arange(half)[:, None] // GROUP == np.arange(half)[None, :] // GROUP)
    pblk = jnp.asarray(pblk, BF16)

    def row(g):
        return g.reshape(1, -1).astype(F32)

    def mixer_weights(t_mlp, tile):
        wsp, bsp = _pair_spatial(w_spatial[0][:, :t_mlp, :t_mlp], b_spatial[0][:, :t_mlp], t_mlp)
        return (row(g_attn[0]), w_in[0].astype(BF16), row(g_v[0]), pblk, wsp, bsp,
                _rel_bias_by_offset(rel_bias[0], tile), row(g_out_a[0]), row(g_out_b[0]), w_out[0].astype(BF16))

    ffn_w = (row(g_ffn[0]), w_gate[0].astype(BF16), w_up[0].astype(BF16), w_down[0].astype(BF16),
             row(g_ple[0]), w_ple_gate[0].astype(BF16), w_ple_proj[0].astype(BF16), row(g_final))

    yp, kp, vp = _layer(x_prompt, p_prompt[0], mixer_weights(MLP_CHUNK, tile_p), ffn_w,
                        tile=tile_p, t_mlp=MLP_CHUNK, tail_rows=r_p, ff_chunk=1408)
    ck = cache_band_k[0].reshape(bs, PAST, d_b)
    cv = cache_band_v[0].reshape(bs, PAST, d_b)
    hs1, ks, vs, va = _mixer(x_sample, ck, cv, mixer_weights(ss, ss),
                             tile=ss, t_mlp=ss, tail_rows=ss, emit_vn=True)
    ys = _ffn(hs1, p_sample[0].reshape(bs * ss, -1), ffn_w, tile_m=bs * ss, ff_chunk=1408)

    def band_rows(tails, b, rows):
        return tails.reshape(1, b, n_heads, GROUP, rows).transpose(0, 1, 4, 2, 3)

    return (yp.reshape(bp, sp, d), ys.reshape(bs, ss, d),
            band_rows(kp, bp, r_p), band_rows(vp, bp, r_p), band_rows(ks, bs, ss), band_rows(vs, bs, ss),
            va.reshape(1, bs, ss, n_groups, GROUP))
```

```python
import functools

import numpy as np
import jax
import jax.numpy as jnp
from jax import lax
from jax.experimental import pallas as pl
from jax.experimental.pallas import tpu as pltpu

CHUNK = 64
N_BAND_CHUNKS = 8
PAST = CHUNK * N_BAND_CHUNKS
MLP_CHUNK = 128
GROUP = 64
REL_CLIP = 64
EPS = 1e-6
NEG_INF = -1e30
LANES = 128
VMEM_LIMIT = 60 * 1024 * 1024

BF16 = jnp.bfloat16
F32 = jnp.float32
_SQRT_2_OVER_PI = float(np.sqrt(2.0 / np.pi).astype(np.float32))
LOG2_E = float(np.log2(np.e))
_WHOLE_VMEM = pl.BlockSpec(memory_space=pltpu.VMEM)
N_MIX_W = 10
N_FFN_W = 8


def _inv_rms(x):
    return lax.rsqrt(jnp.mean(x * x, axis=-1, keepdims=True) + EPS)


def _rms(x, g):
    return x * _inv_rms(x) * g


def _rms_dot(x, g, w):
    return _dot((x * g).astype(BF16), w) * _inv_rms(x)


def _gelu(x):
    half_x = 0.5 * x
    inner = x * (_SQRT_2_OVER_PI + (_SQRT_2_OVER_PI * 0.044715) * (x * x))
    return half_x + half_x * jnp.tanh(inner)


def _dot(a, b):
    return jnp.dot(a, b, preferred_element_type=F32)


def _dot_nt(a, b):
    return lax.dot_general(a, b, (((1,), (1,)), ((), ())), preferred_element_type=F32)


def _even_head_lanes(shape):
    return (lax.broadcasted_iota(jnp.int32, shape, 1) & GROUP) == 0


def _split_heads(x):
    even = _even_head_lanes(x.shape)
    zero = jnp.zeros_like(x)
    return jnp.where(even, x, zero).astype(BF16), jnp.where(even, zero, x).astype(BF16)


def _init_bias(relb_ref, bias_scr, tile):
    n = relb_ref.shape[-1]
    window = PAST + tile
    rc = lax.broadcasted_iota(jnp.int32, (tile, n), 0) >> 6
    cc = lax.broadcasted_iota(jnp.int32, (tile, n), 1) >> 6
    band = (cc >= rc) & (cc <= rc + N_BAND_CHUNKS)
    for h in range(relb_ref.shape[0]):
        base = jnp.broadcast_to(relb_ref[h:h + 1, :] * LOG2_E, (tile, n))
        toep = pltpu.roll(base, 0, 1, stride=1, stride_axis=0)
        bias_scr[h // 2, (h % 2) * tile:(h % 2 + 1) * tile, :] = jnp.where(band, toep, NEG_INF)[:, 0:window]


def _mixer_tile(x, t, w_refs, k_scr, v_scr, bias_scr, kt_ref, vt_ref, vn_ref, *, tile, t_mlp, mask_pad):
    g_attn_ref, w_in_ref, g_v_ref, pblk_ref, wsp_ref, bsp_ref, _, g_oa_ref, g_ob_ref, w_out_ref = w_refs
    d_a, d_b = g_oa_ref.shape[-1], g_ob_ref.shape[-1]
    window = PAST + tile
    row0 = pl.multiple_of(t * tile, tile)

    z = _rms_dot(x, g_attn_ref[...], w_in_ref[...])
    q = z[:, 2 * d_a:2 * d_a + d_b] * (GROUP ** -0.5 * LOG2_E)
    k = z[:, 2 * d_a + d_b:2 * d_a + 2 * d_b]
    vb = z[:, 2 * d_a + 2 * d_b:2 * d_a + 3 * d_b]
    k_scr[pl.ds(PAST + row0, tile), :] = k.astype(BF16)
    v_scr[pl.ds(PAST + row0, tile), :] = vb.astype(BF16)
    kt_ref[...] = k.T
    vt_ref[...] = vb.T

    q_halves = _split_heads(q)
    even_lane = _even_head_lanes((tile, LANES))
    if mask_pad:
        colw = lax.broadcasted_iota(jnp.int32, (1, PAST), 1)
        pen = jnp.where(colw >= PAST - row0, 0.0, NEG_INF).astype(F32)

    def attend(j):
        kw = k_scr[pl.ds(row0, window), j * LANES:(j + 1) * LANES]
        vw = v_scr[pl.ds(row0, window), j * LANES:(j + 1) * LANES]
        qp = jnp.concatenate([qh[:, j * LANES:(j + 1) * LANES] for qh in q_halves], axis=0)
        s = _dot_nt(qp, kw) + bias_scr[j]
        if mask_pad:
            s = jnp.concatenate([s[:, 0:PAST] + pen, s[:, PAST:window]], axis=1)
        m = jnp.max(s, axis=-1, keepdims=True)
        p = jnp.exp2(s - m).astype(BF16)
        ol = _dot(p, jnp.concatenate([vw, jnp.ones_like(vw)], axis=1))
        o = ol[:, 0:LANES] * (1.0 / ol[:, LANES:2 * LANES])
        return jnp.where(even_lane, o[0:tile], o[tile:2 * tile])

    def group_norm(v):
        v2 = (v * v).astype(BF16)
        half = pblk_ref.shape[0]
        ms = jnp.concatenate([_dot(v2[:, c:c + half], pblk_ref[...]) for c in range(0, d_a, half)],
                             axis=1) * (1.0 / GROUP)
        return v * lax.rsqrt(ms + EPS) * g_v_ref[...]

    def spatial_gate(u, vn):
        vn_halves = _split_heads(vn)
        nblk = tile // t_mlp
        wrow = lax.broadcasted_iota(jnp.int32, (t_mlp, 2 * t_mlp), 0)
        wcol = lax.broadcasted_iota(jnp.int32, (t_mlp, 2 * t_mlp), 1)
        tril = jnp.where(wcol >= t_mlp, wcol - t_mlp, wcol) <= wrow
        ya_slabs = []
        for j in range(d_a // LANES):
            w = wsp_ref[j]
            w = jnp.where(tril, w, jnp.zeros_like(w)).astype(BF16)
            cols = [jnp.concatenate([vh[blk * t_mlp:(blk + 1) * t_mlp, j * LANES:(j + 1) * LANES]
                                     for vh in vn_halves], axis=0) for blk in range(nblk)]
            rhs = cols[0] if nblk == 1 else jnp.concatenate(cols, axis=1)
            mixed = _dot(w, rhs)
            parts = []
            for blk in range(nblk):
                ub = u[blk * t_mlp:(blk + 1) * t_mlp, j * LANES:(j + 1) * LANES]
                parts.append(ub * (mixed[:, blk * LANES:(blk + 1) * LANES] + bsp_ref[j]))
            ya_slabs.append(parts[0] if nblk == 1 else jnp.concatenate(parts, axis=0))
        return jnp.concatenate(ya_slabs, axis=1)

    vn = group_norm(_gelu(z[:, d_a:2 * d_a]))
    if vn_ref is not None:
        vn_ref[...] = vn
    ya = spatial_gate(_gelu(z[:, 0:d_a]), vn)
    yb = jnp.concatenate([attend(j) for j in range(d_b // LANES)], axis=1)

    return (x + _rms_dot(ya, g_oa_ref[...], w_out_ref[0:d_a, :])
            + _rms_dot(yb, g_ob_ref[...], w_out_ref[d_a:d_a + d_b, :]))


def _ffn_tile(h, p, w_refs, *, ff_chunk):
    g_ffn_ref, wg_ref, wu_ref, wd_ref, g_ple_ref, wpg_ref, wpp_ref, g_fin_ref = w_refs
    f = (h * g_ffn_ref[...]).astype(BF16)
    r = _inv_rms(h)
    d_ff = wg_ref.shape[1]
    acc = h
    for c0 in range(0, d_ff, ff_chunk):
        gate = _dot(f, wg_ref[:, c0:c0 + ff_chunk]) * r
        up = _dot(f, wu_ref[:, c0:c0 + ff_chunk]) * r
        act = (gate * jax.nn.sigmoid(gate) * up).astype(BF16)
        acc = acc + _dot(act, wd_ref[c0:c0 + ff_chunk, :])
    pgate = jax.nn.sigmoid(_rms_dot(acc, g_ple_ref[...], wpg_ref[...]))
    proj = _dot(p.astype(BF16), wpp_ref[...])
    return _rms(acc + pgate * proj, g_fin_ref[...])


def _fill_past(k_scr, v_scr, ck_ref, cv_ref):
    d_b = k_scr.shape[-1]
    if ck_ref is None:
        k_scr[0:PAST, :] = jnp.zeros((PAST, d_b), BF16)
        v_scr[0:PAST, :] = jnp.zeros((PAST, d_b), BF16)
    else:
        k_scr[0:PAST, :] = ck_ref[0].astype(BF16)
        v_scr[0:PAST, :] = cv_ref[0].astype(BF16)


def _mixer_kernel(*refs, tile, t_mlp, n_t, has_cache, emit_vn):
    it = iter(refs)
    x_ref = next(it)
    ck_ref = next(it) if has_cache else None
    cv_ref = next(it) if has_cache else None
    w_refs = tuple(next(it) for _ in range(N_MIX_W))
    h_ref, kt_ref, vt_ref = (next(it) for _ in range(3))
    vn_ref = next(it) if emit_vn else None
    k_scr, v_scr, bias_scr = (next(it) for _ in range(3))

    i = pl.program_id(0)
    t = lax.rem(i, n_t)

    @pl.when(i == 0)
    def _():
        _init_bias(w_refs[6], bias_scr, tile)

    @pl.when(t == 0)
    def _():
        _fill_past(k_scr, v_scr, ck_ref, cv_ref)

    h_ref[...] = _mixer_tile(x_ref[...], t, w_refs, k_scr, v_scr, bias_scr, kt_ref, vt_ref, vn_ref,
                             tile=tile, t_mlp=t_mlp, mask_pad=not has_cache)


def _ffn_kernel(h_ref, p_ref, *refs, ff_chunk):
    y_ref = refs[N_FFN_W]
    y_ref[...] = _ffn_tile(h_ref[...], p_ref[...], refs[:N_FFN_W], ff_chunk=ff_chunk)


def _layer_kernel(x_ref, p_ref, *refs, tile, t_mlp, n_t, n_tiles, ff_chunk):
    mix_w = refs[:N_MIX_W]
    ffn_w = refs[N_MIX_W:N_MIX_W + N_FFN_W]
    y_ref, kt_ref, vt_ref, k_scr, v_scr, bias_scr, h_scr = refs[N_MIX_W + N_FFN_W:]

    i = pl.program_id(0)
    t = lax.rem(jnp.minimum(i, n_tiles - 1), n_t)

    @pl.when(i == 0)
    def _():
        _init_bias(mix_w[6], bias_scr, tile)
        h_scr[1] = jnp.zeros(h_scr.shape[1:], F32)

    @pl.when(t == 0)
    def _():
        _fill_past(k_scr, v_scr, None, None)

    h_prev = h_scr[lax.rem(i + 1, 2)]
    h_scr[lax.rem(i, 2)] = _mixer_tile(x_ref[...], t, mix_w, k_scr, v_scr, bias_scr, kt_ref, vt_ref, None,
                                       tile=tile, t_mlp=t_mlp, mask_pad=True)
    y_ref[...] = _ffn_tile(h_prev, p_ref[...], ffn_w, ff_chunk=ff_chunk)


def _rel_bias_by_offset(rel_bias, tile):
    n = PAST + 2 * tile
    e = np.arange(n)
    e = np.where(e < PAST + tile, e, e - n)
    idx = np.clip(PAST - e, -REL_CLIP, REL_CLIP) + REL_CLIP
    return rel_bias[:, idx].astype(F32)


def _tail_map(n_t, tail_tiles):
    first_tail = n_t - tail_tiles

    def index(i):
        return i // n_t, jnp.maximum(lax.rem(i, n_t) - first_tail, 0)
    return index


def _scratch(s, tile, d_b, n_heads):
    return [pltpu.VMEM((PAST + s, d_b), BF16), pltpu.VMEM((PAST + s, d_b), BF16),
            pltpu.VMEM((n_heads // 2, 2 * tile, PAST + tile), F32)]


def _mixer(x, cache_k, cache_v, mix_w, *, tile, t_mlp, tail_rows, emit_vn):
    b, s, d = x.shape
    d_a, d_b = mix_w[7].shape[-1], mix_w[8].shape[-1]
    n_heads = mix_w[6].shape[0]
    has_cache = cache_k is not None
    n_t = s // tile
    tail = _tail_map(n_t, tail_rows // tile)

    in_specs = [pl.BlockSpec((tile, d), lambda i: (i, 0))]
    args = [x.reshape(b * s, d)]
    if has_cache:
        in_specs += [pl.BlockSpec((1, PAST, d_b), lambda i: (i // n_t, 0, 0))] * 2
        args += [cache_k, cache_v]
    in_specs += [_WHOLE_VMEM] * N_MIX_W
    args += list(mix_w)

    out_shape = [jax.ShapeDtypeStruct((b * s, d), F32),
                 jax.ShapeDtypeStruct((b * d_b, tail_rows), F32),
                 jax.ShapeDtypeStruct((b * d_b, tail_rows), F32)]
    out_specs = [pl.BlockSpec((tile, d), lambda i: (i, 0)),
                 pl.BlockSpec((d_b, tile), tail), pl.BlockSpec((d_b, tile), tail)]
    if emit_vn:
        out_shape.append(jax.ShapeDtypeStruct((b * s, d_a), F32))
        out_specs.append(pl.BlockSpec((tile, d_a), lambda i: (i, 0)))

    return pl.pallas_call(
        functools.partial(_mixer_kernel, tile=tile, t_mlp=t_mlp, n_t=n_t, has_cache=has_cache, emit_vn=emit_vn),
        grid=(b * n_t,),
        in_specs=in_specs,
        out_specs=out_specs,
        out_shape=out_shape,
        scratch_shapes=_scratch(s, tile, d_b, n_heads),
        compiler_params=pltpu.CompilerParams(dimension_semantics=("arbitrary",), vmem_limit_bytes=VMEM_LIMIT),
        name="mixer",
    )(*args)


def _ffn(h, p, ffn_w, *, tile_m, ff_chunk):
    m, d = h.shape
    d_p = p.shape[-1]
    return pl.pallas_call(
        functools.partial(_ffn_kernel, ff_chunk=ff_chunk),
        grid=(m // tile_m,),
        in_specs=[pl.BlockSpec((tile_m, d), lambda i: (i, 0)),
                  pl.BlockSpec((tile_m, d_p), lambda i: (i, 0))] + [_WHOLE_VMEM] * N_FFN_W,
        out_specs=pl.BlockSpec((tile_m, d), lambda i: (i, 0)),
        out_shape=jax.ShapeDtypeStruct((m, d), F32),
        compiler_params=pltpu.CompilerParams(dimension_semantics=("arbitrary",), vmem_limit_bytes=VMEM_LIMIT),
        name="ffn",
    )(h, p, *ffn_w)


def _layer(x, p, mix_w, ffn_w, *, tile, t_mlp, tail_rows, ff_chunk):
    b, s, d = x.shape
    d_p = p.shape[-1]
    d_b = mix_w[8].shape[-1]
    n_heads = mix_w[6].shape[0]
    n_t = s // tile
    n_tiles = b * n_t
    tail = _tail_map(n_t, tail_rows // tile)

    def mix_tile(i):
        return jnp.minimum(i, n_tiles - 1)

    def ffn_block(i):
        return jnp.maximum(i - 1, 0), 0

    return pl.pallas_call(
        functools.partial(_layer_kernel, tile=tile, t_mlp=t_mlp, n_t=n_t, n_tiles=n_tiles, ff_chunk=ff_chunk),
        grid=(n_tiles + 1,),
        in_specs=[pl.BlockSpec((tile, d), lambda i: (mix_tile(i), 0)), pl.BlockSpec((tile, d_p), ffn_block)]
        + [_WHOLE_VMEM] * (N_MIX_W + N_FFN_W),
        out_specs=[pl.BlockSpec((tile, d), ffn_block),
                   pl.BlockSpec((d_b, tile), lambda i: tail(mix_tile(i))),
                   pl.BlockSpec((d_b, tile), lambda i: tail(mix_tile(i)))],
        out_shape=[jax.ShapeDtypeStruct((b * s, d), F32),
                   jax.ShapeDtypeStruct((b * d_b, tail_rows), F32),
                   jax.ShapeDtypeStruct((b * d_b, tail_rows), F32)],
        scratch_shapes=_scratch(s, tile, d_b, n_heads) + [pltpu.VMEM((2, tile, d), F32)],
        compiler_params=pltpu.CompilerParams(dimension_semantics=("arbitrary",), vmem_limit_bytes=VMEM_LIMIT),
        name="layer",
    )(x.reshape(b * s, d), p.reshape(b * s, d_p), *mix_w, *ffn_w)


def _pair_spatial(w_s, b_s, t):
    g = w_s.shape[0]
    wsp = w_s.reshape(g // 2, 2, t, t).transpose(0, 2, 1, 3).reshape(g // 2, t, 2 * t)
    bsp = jnp.repeat(b_s.reshape(g // 2, 2, t).transpose(0, 2, 1), GROUP, axis=-1)
    return wsp.astype(F32), bsp.astype(F32)


def kernel(x_prompt, x_sample, p_prompt, p_sample, cache_band_k, cache_band_v, g_attn, w_in, g_v, w_spatial, b_spatial, rel_bias, g_out_a, g_out_b, w_out, g_ffn, w_gate, w_up, w_down, g_ple, w_ple_gate, w_ple_proj, g_final):
    depth = w_in.shape[0]
    assert depth == 1, "the final rmsnorm is fused into the (single) layer's ffn stage"
    bp, sp, d = x_prompt.shape
    bs, ss, _ = x_sample.shape
    d_b = g_out_b.shape[-1]
    n_heads = rel_bias.shape[1]
    n_groups = g_v.shape[1]
    r_p = min(PAST, sp)
    tile_p = 256

    half = 2 * LANES
    pblk = (np.arange(half)[:, None] // GROUP == np.arange(half)[None, :] // GROUP)
    pblk = jnp.asarray(pblk, BF16)

    def row(g):
        return g.reshape(1, -1).astype(F32)

    def mixer_weights(t_mlp, tile):
        wsp, bsp = _pair_spatial(w_spatial[0][:, :t_mlp, :t_mlp], b_spatial[0][:, :t_mlp], t_mlp)
        return (row(g_attn[0]), w_in[0].astype(BF16), row(g_v[0]), pblk, wsp, bsp,
                _rel_bias_by_offset(rel_bias[0], tile), row(g_out_a[0]), row(g_out_b[0]), w_out[0].astype(BF16))

    ffn_w = (row(g_ffn[0]), w_gate[0].astype(BF16), w_up[0].astype(BF16), w_down[0].astype(BF16),
             row(g_ple[0]), w_ple_gate[0].astype(BF16), w_ple_proj[0].astype(BF16), row(g_final))

    yp, kp, vp = _layer(x_prompt, p_prompt[0], mixer_weights(MLP_CHUNK, tile_p), ffn_w,
                        tile=tile_p, t_mlp=MLP_CHUNK, tail_rows=r_p, ff_chunk=2816)
    ck = cache_band_k[0].reshape(bs, PAST, d_b)
    cv = cache_band_v[0].reshape(bs, PAST, d_b)
    hs1, ks, vs, va = _mixer(x_sample, ck, cv, mixer_weights(ss, ss),
                             tile=ss, t_mlp=ss, tail_rows=ss, emit_vn=True)
    ys = _ffn(hs1, p_sample[0].reshape(bs * ss, -1), ffn_w, tile_m=bs * ss, ff_chunk=2816)

    def band_rows(tails, b, rows):
        return tails.reshape(1, b, n_heads, GROUP, rows).transpose(0, 1, 4, 2, 3)

    return (yp.reshape(bp, sp, d), ys.reshape(bs, ss, d),
            band_rows(kp, bp, r_p), band_rows(vp, bp, r_p), band_rows(ks, bs, ss), band_rows(vs, bs, ss),
            va.reshape(1, bs, ss, n_groups, GROUP))
```
